```python
import jax, jax.numpy as jnp
from jax import lax
import numpy as np

D_MODEL = 2048
BATCH = 4
SEQ = 4096
DEPTH = 1

CHUNK = 64
MIX_WIDTH = D_MODEL
CONV_CHANNELS = MIX_WIDTH // 2
CONV_WIDTH = 31
HEAD_DIM = 64
ATT_HEADS = (MIX_WIDTH - CONV_CHANNELS) // HEAD_DIM
ATT_WIDTH = ATT_HEADS * HEAD_DIM
LEFT_CHUNKS = 8
BAND = (LEFT_CHUNKS + 1) * CHUNK
MAX_REL = 256
REL_SIZE = (CHUNK - 1) + MAX_REL + 1
IN_WIDTH = 2 * CONV_CHANNELS + 3 * ATT_WIDTH
N_EXPERTS = 64
TOP_K = 8
N_GROUPS = 8
TOPK_GROUPS = 4
EXPERT_HIDDEN = 512
SHARED_HIDDEN = 512
ROUTED_SCALE = 2.5
ROW_BLOCK = 128
LN_EPS = 1e-5
ALPHA = (2 * DEPTH) ** 0.25
BETA = (8 * DEPTH) ** -0.25

kernel_name = "hybrid_conformer_chunkattn_moe_deepnorm"


def layer_norm(x, g, b):
    xf = x.astype(jnp.float32)
    mu = jnp.mean(xf, axis=-1, keepdims=True)
    var = jnp.mean(jnp.square(xf - mu), axis=-1, keepdims=True)
    return ((xf - mu) * lax.rsqrt(var + LN_EPS) * g.astype(jnp.float32) + b.astype(jnp.float32)).astype(x.dtype)


def conformer_conv_group(u, conv_w, conv_b, ln_g, ln_b):
    a, gate = jnp.split(u, 2, axis=-1)
    v = a * jax.nn.sigmoid(gate)
    y = lax.conv_general_dilated(
        v, conv_w[:, None, :].astype(v.dtype), window_strides=(1,),
        padding=[(CONV_WIDTH - 1, 0)],
        dimension_numbers=('NWC', 'WIO', 'NWC'),
        feature_group_count=CONV_CHANNELS) + conv_b
    y = layer_norm(y, ln_g, ln_b)
    return jax.nn.silu(y)


def chunked_relpos_attention(q, k, v, rel_bias):
    B, S, H, Dh = q.shape
    nc = S // CHUNK
    pad = LEFT_CHUNKS * CHUNK
    k_pad = jnp.pad(k, ((0, 0), (pad, 0), (0, 0), (0, 0)))
    v_pad = jnp.pad(v, ((0, 0), (pad, 0), (0, 0), (0, 0)))
    q_chunks = q.reshape(B, nc, CHUNK, H, Dh).transpose(1, 0, 2, 3, 4)
    rel = (pad + jnp.arange(CHUNK))[:, None] - jnp.arange(BAND)[None, :]
    idx = jnp.clip(rel, -(CHUNK - 1), MAX_REL) + (CHUNK - 1)
    bias = rel_bias[:, idx].astype(jnp.float32)
    scale = HEAD_DIM ** -0.5

    def one_chunk(args):
        qc, c = args
        start = c * CHUNK
        kc = lax.dynamic_slice_in_dim(k_pad, start, BAND, axis=1)
        vc = lax.dynamic_slice_in_dim(v_pad, start, BAND, axis=1)
        s = jnp.einsum('bqhd,bkhd->bhqk', qc, kc, preferred_element_type=jnp.float32) * scale + bias
        key_pos = start - pad + jnp.arange(BAND)
        s = jnp.where(key_pos[None, None, None, :] >= 0, s, -jnp.inf)
        p = jax.nn.softmax(s, axis=-1).astype(vc.dtype)
        return jnp.einsum('bhqk,bkhd->bqhd', p, vc)

    out = lax.map(one_chunk, (q_chunks, jnp.arange(nc)))
    return out.transpose(1, 0, 2, 3, 4).reshape(B, S, H * Dh)


def route(h, w_router, router_bias):
    n = h.shape[0]
    logits = jnp.einsum('nd,de->ne', h, w_router, preferred_element_type=jnp.float32)
    scores = jax.nn.sigmoid(logits)
    sel = scores + router_bias.astype(jnp.float32)
    grp = sel.reshape(n, N_GROUPS, N_EXPERTS // N_GROUPS)
    grp_score = jnp.sum(lax.top_k(grp, 2)[0], axis=-1)
    _, gidx = lax.top_k(grp_score, TOPK_GROUPS)
    gmask = jnp.any(gidx[..., None] == jnp.arange(N_GROUPS), axis=-2)
    emask = jnp.repeat(gmask, N_EXPERTS // N_GROUPS, axis=-1)
    _, top_idx = lax.top_k(jnp.where(emask, sel, -jnp.inf), TOP_K)
    w = jnp.take_along_axis(scores, top_idx, axis=-1)
    w = w / jnp.sum(w, axis=-1, keepdims=True) * ROUTED_SCALE
    return top_idx, w


def routed_experts(h, top_idx, gate, w_gate, w_up, w_down):
    n, d = h.shape
    nk = n * TOP_K
    flat_e = top_idx.reshape(nk)
    flat_tok = jnp.arange(nk, dtype=jnp.int32) // TOP_K
    flat_w = gate.reshape(nk)
    order = jnp.argsort(flat_e)
    e_sorted = flat_e[order]
    counts = jnp.bincount(flat_e, length=N_EXPERTS)
    starts = jnp.cumsum(counts) - counts
    padded = (counts + ROW_BLOCK - 1) // ROW_BLOCK * ROW_BLOCK
    pad_ends = jnp.cumsum(padded)
    pad_starts = pad_ends - padded
    dest = pad_starts[e_sorted] + jnp.arange(nk) - starts[e_sorted]
    n_blocks = -(-nk // ROW_BLOCK) + N_EXPERTS
    n_rows = n_blocks * ROW_BLOCK
    row_tok = jnp.full((n_rows,), n, jnp.int32).at[dest].set(flat_tok[order])
    row_w = jnp.zeros((n_rows,), jnp.float32).at[dest].set(flat_w[order])
    block_e = jnp.minimum(
        jnp.searchsorted(pad_ends, jnp.arange(n_blocks) * ROW_BLOCK, side='right'), N_EXPERTS - 1)
    h_pad = jnp.concatenate([h, jnp.zeros((1, d), h.dtype)], axis=0)

    def one_block(args):
        tok, wts, e = args
        xb = h_pad[tok]
        hid = jax.nn.silu(xb @ w_gate[e]) * (xb @ w_up[e])
        return (hid @ w_down[e]) * wts[:, None].astype(h.dtype)

    y = lax.map(one_block, (row_tok.reshape(n_blocks, ROW_BLOCK),
                            row_w.reshape(n_blocks, ROW_BLOCK), block_e))
    y = y.reshape(n_rows, d)
    return jax.ops.segment_sum(y, row_tok, num_segments=n + 1)[:n]


def setup_inputs(seed: int = 0) -> dict:
    key = jax.random.key(seed)
    ks = jax.random.split(key, 24)
    f32 = jnp.float32
    nrm = lambda k, shape, s: (jax.random.normal(k, shape, f32) * s).astype(f32)
    return {
        "x": nrm(ks[0], (BATCH, SEQ, D_MODEL), 1.0),
        "ln_in_g": 1.0 + nrm(ks[1], (D_MODEL,), 0.05),
        "ln_in_b": nrm(ks[2], (D_MODEL,), 0.02),
        "w_in": nrm(ks[3], (DEPTH, D_MODEL, IN_WIDTH), D_MODEL ** -0.5),
        "conv_w": nrm(ks[4], (DEPTH, CONV_WIDTH, CONV_CHANNELS), CONV_WIDTH ** -0.5),
        "conv_b": nrm(ks[5], (DEPTH, CONV_CHANNELS), 0.02),
        "conv_ln_g": 1.0 + nrm(ks[6], (DEPTH, CONV_CHANNELS), 0.05),
        "conv_ln_b": nrm(ks[7], (DEPTH, CONV_CHANNELS), 0.02),
        "rel_bias": nrm(ks[8], (DEPTH, ATT_HEADS, REL_SIZE), 0.2),
        "w_out": nrm(ks[9], (DEPTH, MIX_WIDTH, D_MODEL), MIX_WIDTH ** -0.5 * BETA),
        "ln1_g": 1.0 + nrm(ks[10], (DEPTH, D_MODEL), 0.05),
        "ln1_b": nrm(ks[11], (DEPTH, D_MODEL), 0.02),
        "w_router": nrm(ks[12], (DEPTH, D_MODEL, N_EXPERTS), D_MODEL ** -0.5),
        "router_bias": nrm(ks[13], (DEPTH, N_EXPERTS), 0.01),
        "w_gate": nrm(ks[14], (DEPTH, N_EXPERTS, D_MODEL, EXPERT_HIDDEN), D_MODEL ** -0.5),
        "w_up": nrm(ks[15], (DEPTH, N_EXPERTS, D_MODEL, EXPERT_HIDDEN), D_MODEL ** -0.5),
        "w_down": nrm(ks[16], (DEPTH, N_EXPERTS, EXPERT_HIDDEN, D_MODEL), EXPERT_HIDDEN ** -0.5 * BETA),
        "w_shared_gate": nrm(ks[17], (DEPTH, D_MODEL, SHARED_HIDDEN), D_MODEL ** -0.5),
        "w_shared_up": nrm(ks[18], (DEPTH, D_MODEL, SHARED_HIDDEN), D_MODEL ** -0.5),
        "w_shared_down": nrm(ks[19], (DEPTH, SHARED_HIDDEN, D_MODEL), SHARED_HIDDEN ** -0.5 * BETA),
        "ln2_g": 1.0 + nrm(ks[20], (DEPTH, D_MODEL), 0.05),
        "ln2_b": nrm(ks[21], (DEPTH, D_MODEL), 0.02),
    }


def reference(x, ln_in_g, ln_in_b, w_in, conv_w, conv_b, conv_ln_g, conv_ln_b, rel_bias,
              w_out, ln1_g, ln1_b, w_router, router_bias, w_gate, w_up, w_down,
              w_shared_gate, w_shared_up, w_shared_down, ln2_g, ln2_b):
    B, S, D = x.shape
    h = layer_norm(x, ln_in_g, ln_in_b)
    c2 = 2 * CONV_CHANNELS
    for i in range(DEPTH):
        u = jnp.einsum('bsd,de->bse', h, w_in[i])
        u_conv, q, k, v = jnp.split(u, [c2, c2 + ATT_WIDTH, c2 + 2 * ATT_WIDTH], axis=-1)
        conv_out = conformer_conv_group(u_conv, conv_w[i], conv_b[i], conv_ln_g[i], conv_ln_b[i])
        att_out = chunked_relpos_attention(
            q.reshape(B, S, ATT_HEADS, HEAD_DIM), k.reshape(B, S, ATT_HEADS, HEAD_DIM),
            v.reshape(B, S, ATT_HEADS, HEAD_DIM), rel_bias[i])
        mixed = jnp.einsum('bse,ed->bsd', jnp.concatenate([conv_out, att_out], axis=-1), w_out[i])
        h = layer_norm(ALPHA * h + mixed, ln1_g[i], ln1_b[i])
        hf = h.reshape(B * S, D)
        top_idx, gate = route(hf, w_router[i], router_bias[i])
        y = routed_experts(hf, top_idx, gate, w_gate[i], w_up[i], w_down[i])
        y = y + (jax.nn.silu(hf @ w_shared_gate[i]) * (hf @ w_shared_up[i])) @ w_shared_down[i]
        h = layer_norm(ALPHA * h + y.reshape(B, S, D), ln2_g[i], ln2_b[i])
    return h
```

```python
import functools

import jax
import jax.numpy as jnp
import numpy as np
from jax import lax
from jax.experimental import pallas as pl
from jax.experimental.pallas import tpu as pltpu

CHUNK = 64
CONV_WIDTH = 31
HEAD_DIM = 64
LEFT_CHUNKS = 8
BAND = (LEFT_CHUNKS + 1) * CHUNK
MAX_REL = 256
N_EXPERTS = 64
TOP_K = 8
N_GROUPS = 8
GROUP_SIZE = N_EXPERTS // N_GROUPS
TOPK_GROUPS = 4
ROUTED_SCALE = 2.5
LN_EPS = 1e-5

LANES = 128
SUBLANES = 8
VMEM_LIMIT = 56 * 1024 * 1024

PAD_ROWS = LEFT_CHUNKS * CHUNK
HALO_ROWS = 32
CONV_ROW_CHUNK = 64
EXPERT_ROW_BLOCK = 256
NEG_BIG = -1e30

F32 = jnp.float32
BF16 = jnp.bfloat16
I32 = jnp.int32
U32 = jnp.uint32


def _cparams(semantics, vmem=VMEM_LIMIT):
    return pltpu.CompilerParams(dimension_semantics=semantics, vmem_limit_bytes=vmem)


def _layer_norm(x, g, b):
    mu = jnp.mean(x, axis=-1, keepdims=True)
    xc = x - mu
    var = jnp.mean(xc * xc, axis=-1, keepdims=True)
    return xc * lax.rsqrt(var + LN_EPS) * g + b


def _inproj_kernel(x_ref, g_ref, b_ref, w_ref, h_ref, u_ref, hn_ref, *, apply_ln):
    i = pl.program_id(1)
    j = pl.program_id(2)

    @pl.when(i == 0)
    def _():
        u_ref[...] = jnp.zeros_like(u_ref)

    @pl.when((i > 0) & (j == 0))
    def _():
        x = x_ref[0]
        h = _layer_norm(x, g_ref[...], b_ref[...]) if apply_ln else x
        h_ref[0] = h
        hn_ref[...] = h.astype(BF16)

    @pl.when(i > 0)
    def _():
        u_ref[0] = jnp.dot(hn_ref[...], w_ref[...], preferred_element_type=F32).astype(BF16)


def _inproj(x, g, b, w_bf16, *, apply_ln, tm, tn):
    bsz, s, d = x.shape
    width = w_bf16.shape[1]
    assert PAD_ROWS % tm == 0 and s % tm == 0 and width % tn == 0
    pad_blocks = PAD_ROWS // tm
    grid = (bsz, s // tm + pad_blocks, width // tn)
    row = lambda bi, i, j: (bi, jnp.maximum(i - pad_blocks, 0), 0)
    return pl.pallas_call(
        functools.partial(_inproj_kernel, apply_ln=apply_ln),
        grid=grid,
        in_specs=[
            pl.BlockSpec((1, tm, d), row),
            pl.BlockSpec((1, d), lambda bi, i, j: (0, 0)),
            pl.BlockSpec((1, d), lambda bi, i, j: (0, 0)),
            pl.BlockSpec((d, tn), lambda bi, i, j: (0, j)),
        ],
        out_specs=[
            pl.BlockSpec((1, tm, d), row),
            pl.BlockSpec((1, tm, tn), lambda bi, i, j: (bi, i, j)),
        ],
        out_shape=[
            jax.ShapeDtypeStruct((bsz, s, d), F32),
            jax.ShapeDtypeStruct((bsz, s + PAD_ROWS, width), BF16),
        ],
        scratch_shapes=[pltpu.VMEM((tm, d), BF16)],
        compiler_params=_cparams(("arbitrary", "arbitrary", "arbitrary")),
        name="inproj",
    )(x, g, b, w_bf16)


def _conv_kernel(ah_ref, gh_ref, at_ref, gt_ref, w_ref, cb_ref, lg_ref, lb_ref, o_ref, v_scr, y_scr, *, ts, c):
    v_scr[0:HALO_ROWS, :] = ah_ref[0].astype(F32) * jax.nn.sigmoid(gh_ref[0].astype(F32))
    v_scr[HALO_ROWS:HALO_ROWS + ts, :] = at_ref[0].astype(F32) * jax.nn.sigmoid(gt_ref[0].astype(F32))
    first_tap = HALO_ROWS - (CONV_WIDTH - 1)
    win_rows = CONV_ROW_CHUNK + HALO_ROWS

    def row_chunk(rc, carry):
        r0 = pl.multiple_of(rc * CONV_ROW_CHUNK, CONV_ROW_CHUNK)
        for ci in range(c // LANES):
            lanes = slice(ci * LANES, (ci + 1) * LANES)
            win = v_scr[pl.ds(r0, win_rows), lanes]
            wc = w_ref[:, lanes]
            acc = jnp.zeros((CONV_ROW_CHUNK, LANES), F32)
            for k in range(CONV_WIDTH):
                acc = acc + wc[k:k + 1, :] * win[first_tap + k:first_tap + k + CONV_ROW_CHUNK, :]
            y_scr[pl.ds(r0, CONV_ROW_CHUNK), lanes] = acc + cb_ref[:, lanes]
        return carry

    lax.fori_loop(0, ts // CONV_ROW_CHUNK, row_chunk, 0)
    y = _layer_norm(y_scr[...], lg_ref[...], lb_ref[...])
    o_ref[0] = (y * jax.nn.sigmoid(y)).astype(BF16)


def _conv(u, conv_w, conv_b, ln_g, ln_b, *, s, c, ts):
    bsz = u.shape[0]
    assert PAD_ROWS % ts == 0 and s % ts == 0 and ts % HALO_ROWS == 0 and ts % CONV_ROW_CHUNK == 0
    hb = ts // HALO_ROWS
    pb = PAD_ROWS // ts
    halo = lambda col: pl.BlockSpec((1, HALO_ROWS, c), lambda bi, i: (bi, (pb + i) * hb - 1, col))
    tile = lambda col: pl.BlockSpec((1, ts, c), lambda bi, i: (bi, pb + i, col))
    vec = lambda rows: pl.BlockSpec((rows, c), lambda bi, i: (0, 0))
    return pl.pallas_call(
        functools.partial(_conv_kernel, ts=ts, c=c),
        grid=(bsz, s // ts),
        in_specs=[halo(0), halo(1), tile(0), tile(1), vec(CONV_WIDTH), vec(1), vec(1), vec(1)],
        out_specs=pl.BlockSpec((1, ts, c), lambda bi, i: (bi, i, 0)),
        out_shape=jax.ShapeDtypeStruct((bsz, s, c), BF16),
        scratch_shapes=[pltpu.VMEM((ts + HALO_ROWS, c), F32), pltpu.VMEM((ts, c), F32)],
        compiler_params=_cparams(("arbitrary", "arbitrary")),
        name="conv",
    )(u, u, u, u, conv_w, conv_b, ln_g, ln_b)


def _attn_kernel(q_ref, k0_ref, k1_ref, k2_ref, v0_ref, v1_ref, v2_ref, bias_ref, o_ref, kw_scr, vw_scr, *, qb, aw):
    i = pl.program_id(1)
    for j, (kr, vr) in enumerate(((k0_ref, v0_ref), (k1_ref, v1_ref), (k2_ref, v2_ref))):
        kw_scr[j * qb:(j + 1) * qb, :] = kr[0]
        vw_scr[j * qb:(j + 1) * qb, :] = vr[0]
    lane = lax.broadcasted_iota(I32, (1, LANES), 1)
    first_head = lane < HEAD_DIM
    key_iota = lax.broadcasted_iota(I32, (1, BAND), 1)
    scale = HEAD_DIM ** -0.5
    zero = jnp.zeros((), BF16)

    def head_pair(hp, carry):
        lo = pl.multiple_of(hp * LANES, LANES)
        qp = q_ref[0, :, pl.ds(lo, LANES)] * jnp.asarray(scale, BF16)
        kp = kw_scr[:, pl.ds(lo, LANES)]
        vp = vw_scr[:, pl.ds(lo, LANES)]
        qm = (jnp.where(first_head, qp, zero), jnp.where(first_head, zero, qp))
        vm = (jnp.where(first_head, vp, zero), jnp.where(first_head, zero, vp))
        for ca in range(qb // CHUNK):
            rows = slice(ca * CHUNK, (ca + 1) * CHUNK)
            band = slice(ca * CHUNK, ca * CHUNK + BAND)
            valid = (i * qb + ca * CHUNK + key_iota) >= PAD_ROWS
            kc = kp[band]
            out = jnp.zeros((CHUNK, LANES), F32)
            for par in range(2):
                sc = lax.dot_general(qm[par][rows], kc, (((1,), (1,)), ((), ())), preferred_element_type=F32)
                sc = sc + bias_ref[2 * hp + par]
                sc = jnp.where(valid, sc, NEG_BIG)
                m = jnp.max(sc, axis=-1, keepdims=True)
                e = jnp.exp(sc - m)
                l = jnp.sum(e, axis=-1, keepdims=True)
                p = (e * (1.0 / l)).astype(BF16)
                out = out + jnp.dot(p, vm[par][band], preferred_element_type=F32)
            o_ref[0, rows, pl.ds(lo, LANES)] = out.astype(BF16)
        return carry

    lax.fori_loop(0, aw // LANES, head_pair, 0)


def _attention(u, bias, *, s, aw, qb, qcol):
    bsz = u.shape[0]
    assert PAD_ROWS == 2 * qb and s % qb == 0 and aw % LANES == 0
    pb = PAD_ROWS // qb
    kv = lambda col, j: pl.BlockSpec((1, qb, aw), lambda bi, i: (bi, i + j, col))
    return pl.pallas_call(
        functools.partial(_attn_kernel, qb=qb, aw=aw),
        grid=(bsz, s // qb),
        in_specs=[pl.BlockSpec((1, qb, aw), lambda bi, i: (bi, pb + i, qcol))]
        + [kv(qcol + 1, j) for j in range(3)] + [kv(qcol + 2, j) for j in range(3)]
        + [pl.BlockSpec(bias.shape, lambda bi, i: (0, 0, 0))],
        out_specs=pl.BlockSpec((1, qb, aw), lambda bi, i: (bi, i, 0)),
        out_shape=jax.ShapeDtypeStruct((bsz, s, aw), BF16),
        scratch_shapes=[pltpu.VMEM((3 * qb, aw), BF16), pltpu.VMEM((3 * qb, aw), BF16)],
        compiler_params=_cparams(("arbitrary", "arbitrary")),
        name="attn",
    )(u, u, u, u, u, u, u, bias)


def _expand_rel_bias(rel_bias):
    rel = (PAD_ROWS + np.arange(CHUNK))[:, None] - np.arange(BAND)[None, :]
    idx = np.clip(rel, -(CHUNK - 1), MAX_REL) + (CHUNK - 1)
    return rel_bias[:, idx].astype(F32)


def _pack_rows(h, ref, *, rows, d):
    half = d // 2
    for sidx in range(half // LANES):
        lo = h[:, sidx * LANES:(sidx + 1) * LANES].astype(BF16).astype(F32)
        hi = h[:, half + sidx * LANES:half + (sidx + 1) * LANES].astype(BF16).astype(F32)
        word = (pltpu.bitcast(lo, U32) >> 16) | (pltpu.bitcast(hi, U32) & jnp.uint32(0xFFFF0000))
        ref[pl.ds(sidx, rows, stride=half // LANES), :] = word


def _unpack_rows(ref, x_scr, *, rows, d):
    half = d // 2
    for sidx in range(half // LANES):
        word = ref[pl.ds(sidx, rows, stride=half // LANES), :]
        lo = pltpu.bitcast(word << 16, F32)
        hi = pltpu.bitcast(word & jnp.uint32(0xFFFF0000), F32)
        x_scr[:, sidx * LANES:(sidx + 1) * LANES] = lo.astype(BF16)
        x_scr[:, half + sidx * LANES:half + (sidx + 1) * LANES] = hi.astype(BF16)


def _outproj_kernel(conv_ref, att_ref, h_ref, wo_ref, g_ref, b_ref, wr_ref, wsg_ref, wsu_ref, wsd_ref,
                    base_ref, h1p_ref, lg_ref, *, alpha, c, tm, d):
    mixed = jnp.dot(conv_ref[...], wo_ref[0:c, :], preferred_element_type=F32)
    mixed = mixed + jnp.dot(att_ref[...], wo_ref[c:, :], preferred_element_type=F32)
    h1 = _layer_norm(alpha * h_ref[...] + mixed, g_ref[...], b_ref[...])
    h1b = h1.astype(BF16)
    lg_ref[...] = lax.dot_general(wr_ref[...], h1b, (((1,), (1,)), ((), ())), preferred_element_type=F32)
    sg = jnp.dot(h1b, wsg_ref[...], preferred_element_type=F32)
    su = jnp.dot(h1b, wsu_ref[...], preferred_element_type=F32)
    hid = (sg * jax.nn.sigmoid(sg) * su).astype(BF16)
    base_ref[...] = alpha * h1 + jnp.dot(hid, wsd_ref[...], preferred_element_type=F32)
    _pack_rows(h1, h1p_ref, rows=tm, d=d)


def _outproj(conv_out, att_out, h, wo, g, b, wr_t, wsg, wsu, wsd, *, alpha, tm):
    n, d = h.shape
    c = conv_out.shape[1]
    fs = wsg.shape[1]
    rpt = d // 2 // LANES
    const = lambda shape: pl.BlockSpec(shape, lambda i: (0,) * len(shape), pipeline_mode=pl.Buffered(1))
    return pl.pallas_call(
        functools.partial(_outproj_kernel, alpha=alpha, c=c, tm=tm, d=d),
        grid=(n // tm,),
        in_specs=[
            pl.BlockSpec((tm, c), lambda i: (i, 0)),
            pl.BlockSpec((tm, att_out.shape[1]), lambda i: (i, 0)),
            pl.BlockSpec((tm, d), lambda i: (i, 0)),
            const(wo.shape), const((1, d)), const((1, d)), const(wr_t.shape),
            const((d, fs)), const((d, fs)), const((fs, d)),
        ],
        out_specs=[
            pl.BlockSpec((tm, d), lambda i: (i, 0)),
            pl.BlockSpec((tm * rpt, LANES), lambda i: (i, 0)),
            pl.BlockSpec((N_EXPERTS, tm), lambda i: (0, i)),
        ],
        out_shape=[
            jax.ShapeDtypeStruct((n, d), F32),
            jax.ShapeDtypeStruct((n * rpt, LANES), U32),
            jax.ShapeDtypeStruct((N_EXPERTS, n), F32),
        ],
        compiler_params=_cparams(("arbitrary",)),
        name="outproj",
    )(conv_out, att_out, h, wo, g, b, wr_t, wsg, wsu, wsd)


def _router_kernel(lg_ref, rb_ref, tri_ref, pos_ref, gd_ref, cnt_ref, carry_scr, *, tt):
    i = pl.program_id(0)

    @pl.when(i == 0)
    def _():
        carry_scr[...] = jnp.zeros_like(carry_scr)

    scores = jax.nn.sigmoid(lg_ref[...])
    sel = scores + rb_ref[...]
    sel3 = sel.reshape(GROUP_SIZE, N_GROUPS, tt)
    jio = lax.broadcasted_iota(I32, (GROUP_SIZE, N_GROUPS, tt), 0)
    m1 = jnp.max(sel3, axis=0)
    first = jnp.min(jnp.where(sel3 == m1[None], jio, GROUP_SIZE), axis=0)
    m2 = jnp.max(jnp.where(jio == first[None], -jnp.inf, sel3), axis=0)
    gs = m1 + m2
    gio = lax.broadcasted_iota(I32, (N_GROUPS, tt), 0)
    grank = jnp.zeros((N_GROUPS, tt), I32)
    for gp in range(N_GROUPS):
        row = gs[gp:gp + 1, :]
        tie = jnp.where(gio > gp, 1, 0)
        grank = grank + jnp.where(row > gs, 1, jnp.where(row == gs, tie, 0))
    gmask = grank < TOPK_GROUPS
    masked = jnp.where(gmask[None], sel3, -jnp.inf).reshape(N_EXPERTS, tt)
    rio = lax.broadcasted_iota(I32, (N_EXPERTS, tt), 0)
    eid = (rio % N_GROUPS) * GROUP_SIZE + rio // N_GROUPS
    erank = jnp.zeros((N_EXPERTS, tt), I32)
    for rp in range(N_EXPERTS):
        ep = (rp % N_GROUPS) * GROUP_SIZE + rp // N_GROUPS
        row = masked[rp:rp + 1, :]
        tie = jnp.where(eid > ep, 1, 0)
        erank = erank + jnp.where(row > masked, 1, jnp.where(row == masked, tie, 0))
    smask = erank < TOP_K
    picked = jnp.where(smask, scores, 0.0)
    wsum = jnp.sum(picked, axis=0, keepdims=True)
    gd_ref[...] = picked / wsum * ROUTED_SCALE
    ones = jnp.where(smask, 1.0, 0.0)
    incl = jnp.dot(ones.astype(BF16), tri_ref[...], preferred_element_type=F32)
    pos = carry_scr[...] + incl - ones
    pos_ref[...] = jnp.where(smask, pos, -1.0).astype(I32)
    carry = carry_scr[...] + jnp.sum(ones, axis=1, keepdims=True)
    carry_scr[...] = carry
    cnt_ref[...] = jnp.broadcast_to(carry, (N_EXPERTS, LANES)).astype(I32)


def _router(logits_t, rb, tri, *, tt):
    n = logits_t.shape[1]
    tile = pl.BlockSpec((N_EXPERTS, tt), lambda i: (0, i))
    return pl.pallas_call(
        functools.partial(_router_kernel, tt=tt),
        grid=(n // tt,),
        in_specs=[tile, pl.BlockSpec((N_EXPERTS, 1), lambda i: (0, 0)), pl.BlockSpec((tt, tt), lambda i: (0, 0))],
        out_specs=[tile, tile, pl.BlockSpec((N_EXPERTS, LANES), lambda i: (0, 0))],
        out_shape=[
            jax.ShapeDtypeStruct((N_EXPERTS, n), I32),
            jax.ShapeDtypeStruct((N_EXPERTS, n), F32),
            jax.ShapeDtypeStruct((N_EXPERTS, LANES), I32),
        ],
        scratch_shapes=[pltpu.VMEM((N_EXPERTS, 1), F32)],
        compiler_params=_cparams(("arbitrary",)),
        name="router",
    )(logits_t, rb, tri)


def _slots_kernel(pos_ref, gd_ref, ps_ref, ltri_ref, dest_ref, gate_ref):
    pos = pos_ref[...]
    gd = gd_ref[...]
    chosen = pos >= 0
    ones = jnp.where(chosen, 1.0, 0.0).astype(BF16)
    before = jnp.dot(ltri_ref[...], ones, preferred_element_type=F32)
    dest_full = pos + ps_ref[...]
    for k in range(TOP_K):
        hit = jnp.where(chosen, before, -1.0) == float(k)
        dest_ref[k:k + 1, :] = jnp.sum(jnp.where(hit, dest_full, 0), axis=0, keepdims=True)
        gate_ref[k:k + 1, :] = jnp.sum(jnp.where(hit, gd, 0.0), axis=0, keepdims=True)


def _slots(pos, gd, pad_start, ltri, *, tt):
    n = pos.shape[1]
    tile = pl.BlockSpec((N_EXPERTS, tt), lambda i: (0, i))
    otile = pl.BlockSpec((TOP_K, tt), lambda i: (0, i))
    return pl.pallas_call(
        _slots_kernel,
        grid=(n // tt,),
        in_specs=[tile, tile, pl.BlockSpec((N_EXPERTS, 1), lambda i: (0, 0)),
                  pl.BlockSpec((N_EXPERTS, N_EXPERTS), lambda i: (0, 0))],
        out_specs=[otile, otile],
        out_shape=[jax.ShapeDtypeStruct((TOP_K, n), I32), jax.ShapeDtypeStruct((TOP_K, n), F32)],
        compiler_params=_cparams(("arbitrary",)),
        name="slots",
    )(pos, gd, pad_start, ltri)


def _dispatch_kernel(pad_end_ref, cnt_ref, dest_ref, h1p_ref, xs_ref, zero_scr, zsem, sem, *, td, rpt, rb):
    i = pl.program_id(0)

    def zero_copy(r):
        row0 = pl.multiple_of((pad_end_ref[r] - rb) * rpt, SUBLANES)
        return pltpu.make_async_copy(zero_scr, xs_ref.at[pl.ds(row0, rb * rpt)], zsem)

    @pl.when(i == 0)
    def _():
        zero_scr[...] = jnp.zeros_like(zero_scr)

        def start(r, carry):
            @pl.when(cnt_ref[r] > 0)
            def _():
                zero_copy(r).start()
            return carry

        def wait(r, carry):
            @pl.when(cnt_ref[r] > 0)
            def _():
                zero_copy(r).wait()
            return carry

        lax.fori_loop(0, N_EXPERTS, start, 0)
        lax.fori_loop(0, N_EXPERTS, wait, 0)

    def token(t, carry):
        src = h1p_ref.at[pl.ds(pl.multiple_of(t * rpt, SUBLANES), rpt)]
        for k in range(TOP_K):
            row0 = pl.multiple_of(dest_ref[t * TOP_K + k] * rpt, SUBLANES)
            pltpu.make_async_copy(src, xs_ref.at[pl.ds(row0, rpt)], sem).start()
        return carry

    lax.fori_loop(0, td, token, 0)
    for k in range(TOP_K):
        pltpu.make_async_copy(h1p_ref, xs_ref.at[pl.ds(0, td * rpt)], sem).wait()


def _dispatch(pad_end, counts, dest_flat, h1p, *, n_rows, td, rpt, rb):
    n = h1p.shape[0] // rpt
    grid_spec = pltpu.PrefetchScalarGridSpec(
        num_scalar_prefetch=2,
        grid=(n // td,),
        in_specs=[
            pl.BlockSpec((td * TOP_K,), lambda i, pe, cn: (i,), memory_space=pltpu.SMEM),
            pl.BlockSpec((td * rpt, LANES), lambda i, pe, cn: (i, 0)),
        ],
        out_specs=pl.BlockSpec(memory_space=pl.ANY),
        scratch_shapes=[pltpu.VMEM((rb * rpt, LANES), U32), pltpu.SemaphoreType.DMA(()), pltpu.SemaphoreType.DMA(())],
    )
    return pl.pallas_call(
        functools.partial(_dispatch_kernel, td=td, rpt=rpt, rb=rb),
        grid_spec=grid_spec,
        out_shape=jax.ShapeDtypeStruct((n_rows * rpt, LANES), U32),
        compiler_params=_cparams(("arbitrary",)),
        name="dispatch",
    )(pad_end, counts, dest_flat, h1p)


def _expert_kernel(bidx_ref, bexp_ref, nused_ref, xs_ref, wg_ref, wu_ref, wd_ref, y_ref, x_scr, *, rb, d):
    @pl.when(pl.program_id(0) < nused_ref[0])
    def _():
        _unpack_rows(xs_ref, x_scr, rows=rb, d=d)
        x = x_scr[...]
        g = jnp.dot(x, wg_ref[0], preferred_element_type=F32)
        u = jnp.dot(x, wu_ref[0], preferred_element_type=F32)
        hid = (g * jax.nn.sigmoid(g) * u).astype(BF16)
        y = jnp.dot(hid, wd_ref[0], preferred_element_type=F32)
        for ci in range(d // LANES):
            y_ref[pl.ds(ci, rb, stride=d // LANES), :] = y[:, ci * LANES:(ci + 1) * LANES]


def _experts(bidx, bexp, nused, xs, wg, wu, wd, *, rb, d):
    rpt = d // 2 // LANES
    ypt = d // LANES
    n_blocks = xs.shape[0] // (rb * rpt)
    f = wg.shape[2]
    grid_spec = pltpu.PrefetchScalarGridSpec(
        num_scalar_prefetch=3,
        grid=(n_blocks,),
        in_specs=[
            pl.BlockSpec((rb * rpt, LANES), lambda b, bi, be, nu: (bi[b], 0)),
            pl.BlockSpec((1, d, f), lambda b, bi, be, nu: (be[b], 0, 0)),
            pl.BlockSpec((1, d, f), lambda b, bi, be, nu: (be[b], 0, 0)),
            pl.BlockSpec((1, f, d), lambda b, bi, be, nu: (be[b], 0, 0)),
        ],
        out_specs=pl.BlockSpec((rb * ypt, LANES), lambda b, bi, be, nu: (bi[b], 0)),
        scratch_shapes=[pltpu.VMEM((rb, d), BF16)],
    )
    return pl.pallas_call(
        functools.partial(_expert_kernel, rb=rb, d=d),
        grid_spec=grid_spec,
        out_shape=jax.ShapeDtypeStruct((n_blocks * rb * ypt, LANES), F32),
        compiler_params=_cparams(("arbitrary",)),
        name="experts",
    )(bidx, bexp, nused, xs, wg, wu, wd)


def _combine_kernel(dest_ref, gate_ref, base_ref, g_ref, b_ref, y_ref, o_ref, buf, r_scr, sem, *, tc, ypt, nt):
    i = pl.program_id(0)
    slot = i % 2

    @pl.when(i < nt)
    def _():
        def token(t, carry):
            for k in range(TOP_K):
                row0 = pl.multiple_of(dest_ref[t * TOP_K + k] * ypt, ypt)
                dst = buf.at[slot * TOP_K + k, pl.ds(pl.multiple_of(t * ypt, ypt), ypt)]
                pltpu.make_async_copy(y_ref.at[pl.ds(row0, ypt)], dst, sem.at[slot]).start()
            return carry

        lax.fori_loop(0, tc, token, 0)

    @pl.when(i > 0)
    def _():
        ps = 1 - slot
        for k in range(TOP_K):
            pltpu.make_async_copy(y_ref.at[pl.ds(0, tc * ypt)], buf.at[ps * TOP_K + k], sem.at[ps]).wait()
        gate = gate_ref[...]
        gk = [jnp.broadcast_to(gate[:, k:k + 1], (tc, LANES)) for k in range(TOP_K)]
        for ci in range(ypt):
            lanes = slice(ci * LANES, (ci + 1) * LANES)
            acc = base_ref[:, lanes]
            for k in range(TOP_K):
                acc = acc + gk[k] * buf[ps * TOP_K + k, pl.ds(ci, tc, stride=ypt), :]
            r_scr[:, lanes] = acc
        o_ref[...] = _layer_norm(r_scr[...], g_ref[...], b_ref[...])


def _combine(dest_flat, gate_t, base, g, b, y, *, tc):
    n, d = base.shape
    ypt = d // LANES
    nt = n // tc
    prev = lambda i: (jnp.maximum(i - 1, 0), 0)
    return pl.pallas_call(
        functools.partial(_combine_kernel, tc=tc, ypt=ypt, nt=nt),
        grid=(nt + 1,),
        in_specs=[
            pl.BlockSpec((tc * TOP_K,), lambda i: (jnp.minimum(i, nt - 1),), memory_space=pltpu.SMEM),
            pl.BlockSpec((tc, TOP_K), prev),
            pl.BlockSpec((tc, d), prev),
            pl.BlockSpec((1, d), lambda i: (0, 0)),
            pl.BlockSpec((1, d), lambda i: (0, 0)),
            pl.BlockSpec(memory_space=pl.ANY),
        ],
        out_specs=pl.BlockSpec((tc, d), prev),
        out_shape=jax.ShapeDtypeStruct((n, d), F32),
        scratch_shapes=[
            pltpu.VMEM((2 * TOP_K, tc * ypt, LANES), F32),
            pltpu.VMEM((tc, d), F32),
            pltpu.SemaphoreType.DMA((2,)),
        ],
        compiler_params=_cparams(("arbitrary",)),
        name="combine",
    )(dest_flat, gate_t, base, g, b, y)


def _row_vec(v):
    return v.reshape(1, -1).astype(F32)


def _tile(n, want):
    t = min(n, want)
    assert n % t == 0
    return t


def kernel(x, ln_in_g, ln_in_b, w_in, conv_w, conv_b, conv_ln_g, conv_ln_b, rel_bias, w_out, ln1_g, ln1_b,
           w_router, router_bias, w_gate, w_up, w_down, w_shared_gate, w_shared_up, w_shared_down, ln2_g, ln2_b):
    bsz, s, d = x.shape
    n = bsz * s
    depth = w_in.shape[0]
    c = conv_w.shape[2]
    aw = rel_bias.shape[1] * HEAD_DIM
    assert c == aw and w_in.shape[2] == 2 * c + 3 * aw and w_router.shape[2] == N_EXPERTS
    alpha = (2 * depth) ** 0.25
    rpt = d // 2 // LANES
    rb = EXPERT_ROW_BLOCK
    n_blocks = -(-n * TOP_K // rb) + N_EXPERTS
    n_rows = n_blocks * rb
    tt = _tile(n, 1024)
    tri = jnp.triu(jnp.ones((tt, tt), BF16))
    ltri = jnp.tril(jnp.ones((N_EXPERTS, N_EXPERTS), BF16), -1)
    rows = np.arange(N_EXPERTS)
    row_expert = (rows % N_GROUPS) * GROUP_SIZE + rows // N_GROUPS

    h = x
    for li in range(depth):
        hn, u = _inproj(h, _row_vec(ln_in_g), _row_vec(ln_in_b), w_in[li].astype(BF16),
                        apply_ln=(li == 0), tm=_tile(s, 512), tn=_tile(w_in.shape[2], 1024))
        conv_out = _conv(u, conv_w[li].astype(F32), _row_vec(conv_b[li]), _row_vec(conv_ln_g[li]),
                         _row_vec(conv_ln_b[li]), s=s, c=c, ts=_tile(s, 512))
        att_out = _attention(u, _expand_rel_bias(rel_bias[li]), s=s, aw=aw, qb=PAD_ROWS // 2, qcol=2)
        wr_t = w_router[li].T[row_expert].astype(BF16)
        base, h1p, logits_t = _outproj(
            conv_out.reshape(n, c), att_out.reshape(n, aw), hn.reshape(n, d), w_out[li].astype(BF16),
            _row_vec(ln1_g[li]), _row_vec(ln1_b[li]), wr_t, w_shared_gate[li].astype(BF16),
            w_shared_up[li].astype(BF16), w_shared_down[li].astype(BF16), alpha=alpha, tm=_tile(n, 512))
        rbias = router_bias[li][row_expert].reshape(N_EXPERTS, 1).astype(F32)
        pos, gd, cnt = _router(logits_t, rbias, tri, tt=tt)
        counts = cnt[:, 0]
        padded = (counts + rb - 1) // rb * rb
        pad_end = jnp.cumsum(padded).astype(I32)
        pad_start = pad_end - padded
        dest, gate = _slots(pos, gd, pad_start.reshape(N_EXPERTS, 1), ltri, tt=tt)
        dest_flat = dest.T.reshape(n * TOP_K)
        xs = _dispatch(pad_end, counts, dest_flat, h1p, n_rows=n_rows, td=_tile(n, 256), rpt=rpt, rb=rb)
        n_used = pad_end[-1] // rb
        blk = jnp.minimum(jnp.arange(n_blocks, dtype=I32), n_used - 1)
        blk_row = jnp.minimum(jnp.searchsorted(pad_end, blk * rb, side='right'), N_EXPERTS - 1)
        blk_exp = jnp.asarray(row_expert, I32)[blk_row]
        y = _experts(blk, blk_exp, n_used.reshape(1), xs, w_gate[li].astype(BF16), w_up[li].astype(BF16),
                     w_down[li].astype(BF16), rb=rb, d=d)
        h = _combine(dest_flat, gate.T, base, _row_vec(ln2_g[li]), _row_vec(ln2_b[li]), y,
                     tc=_tile(n, 128)).reshape(bsz, s, d)
    return h
```

```python
import functools

import jax
import jax.numpy as jnp
from jax import lax
from jax.experimental import pallas as pl
from jax.experimental.pallas import tpu as pltpu

CHUNK = 64
CONV_WIDTH = 31
HEAD_DIM = 64
LEFT_CHUNKS = 8
BAND = (LEFT_CHUNKS + 1) * CHUNK
MAX_REL = 256
N_EXPERTS = 64
TOP_K = 8
N_GROUPS = 8
GROUP_SIZE = N_EXPERTS // N_GROUPS
TOPK_GROUPS = 4
ROUTED_SCALE = 2.5
LN_EPS = 1e-5

LANES = 128
SUBLANES = 8
VMEM_LIMIT = 56 * 1024 * 1024

PAD_ROWS = LEFT_CHUNKS * CHUNK
HALO_ROWS = 32
CONV_ROW_CHUNK = 64
EXPERT_ROW_BLOCK = 256
NEG_BIG = -1e30

F32 = jnp.float32
BF16 = jnp.bfloat16
I32 = jnp.int32


def _cparams(semantics, vmem=VMEM_LIMIT):
    return pltpu.CompilerParams(dimension_semantics=semantics, vmem_limit_bytes=vmem)


def _layer_norm(x, g, b):
    mu = jnp.mean(x, axis=-1, keepdims=True)
    xc = x - mu
    var = jnp.mean(xc * xc, axis=-1, keepdims=True)
    return xc * lax.rsqrt(var + LN_EPS) * g + b


def _inproj_kernel(x_ref, g_ref, b_ref, w_ref, wvt_ref, h_ref, u_ref, vt_ref, hn_ref, *, apply_ln, n_u):
    i = pl.program_id(1)
    j = pl.program_id(2)

    @pl.when(i == 0)
    def _():
        u_ref[...] = jnp.zeros_like(u_ref)
        vt_ref[...] = jnp.zeros_like(vt_ref)

    @pl.when((i > 0) & (j == 0))
    def _():
        x = x_ref[0]
        h = _layer_norm(x, g_ref[...], b_ref[...]) if apply_ln else x
        h_ref[0] = h
        hn_ref[...] = h.astype(BF16)

    @pl.when((i > 0) & (j < n_u))
    def _():
        u_ref[0] = jnp.dot(hn_ref[...], w_ref[...], preferred_element_type=F32).astype(BF16)

    @pl.when((i > 0) & (j == n_u))
    def _():
        vt = lax.dot_general(wvt_ref[...], hn_ref[...], (((1,), (1,)), ((), ())), preferred_element_type=F32)
        vt_ref[0] = vt.astype(BF16)


def _inproj(x, g, b, w_u, w_vt, *, apply_ln, tm, tn):
    bsz, s, d = x.shape
    width = w_u.shape[1]
    aw = w_vt.shape[0]
    assert PAD_ROWS % tm == 0 and s % tm == 0 and width % tn == 0
    pad_blocks = PAD_ROWS // tm
    n_u = width // tn
    grid = (bsz, s // tm + pad_blocks, n_u + 1)
    row = lambda bi, i, j: (bi, jnp.maximum(i - pad_blocks, 0), 0)
    const = lambda shape: pl.BlockSpec(shape, lambda bi, i, j: (0,) * len(shape), pipeline_mode=pl.Buffered(1))
    return pl.pallas_call(
        functools.partial(_inproj_kernel, apply_ln=apply_ln, n_u=n_u),
        grid=grid,
        in_specs=[
            pl.BlockSpec((1, tm, d), row),
            const((1, d)), const((1, d)),
            pl.BlockSpec((d, tn), lambda bi, i, j: (0, jnp.minimum(j, n_u - 1))),
            const((aw, d)),
        ],
        out_specs=[
            pl.BlockSpec((1, tm, d), row),
            pl.BlockSpec((1, tm, tn), lambda bi, i, j: (bi, i, jnp.minimum(j, n_u - 1))),
            pl.BlockSpec((1, aw, tm), lambda bi, i, j: (bi, 0, i)),
        ],
        out_shape=[
            jax.ShapeDtypeStruct((bsz, s, d), F32),
            jax.ShapeDtypeStruct((bsz, s + PAD_ROWS, width), BF16),
            jax.ShapeDtypeStruct((bsz, aw, s + PAD_ROWS), BF16),
        ],
        scratch_shapes=[pltpu.VMEM((tm, d), BF16)],
        compiler_params=_cparams(("arbitrary", "arbitrary", "arbitrary")),
        name="inproj",
    )(x, g, b, w_u, w_vt)


def _conv_kernel(ah_ref, gh_ref, at_ref, gt_ref, w_ref, cb_ref, lg_ref, lb_ref, o_ref, v_scr, y_scr, *, ts, c):
    v_scr[0:HALO_ROWS, :] = ah_ref[0].astype(F32) * jax.nn.sigmoid(gh_ref[0].astype(F32))
    v_scr[HALO_ROWS:HALO_ROWS + ts, :] = at_ref[0].astype(F32) * jax.nn.sigmoid(gt_ref[0].astype(F32))
    first_tap = HALO_ROWS - (CONV_WIDTH - 1)
    win_rows = CONV_ROW_CHUNK + HALO_ROWS

    def row_chunk(rc, carry):
        r0 = pl.multiple_of(rc * CONV_ROW_CHUNK, CONV_ROW_CHUNK)
        for ci in range(c // LANES):
            lanes = slice(ci * LANES, (ci + 1) * LANES)
            win = v_scr[pl.ds(r0, win_rows), lanes]
            wc = w_ref[:, lanes]
            acc = jnp.zeros((CONV_ROW_CHUNK, LANES), F32)
            for k in range(CONV_WIDTH):
                acc = acc + wc[k:k + 1, :] * win[first_tap + k:first_tap + k + CONV_ROW_CHUNK, :]
            y_scr[pl.ds(r0, CONV_ROW_CHUNK), lanes] = acc + cb_ref[:, lanes]
        return carry

    lax.fori_loop(0, ts // CONV_ROW_CHUNK, row_chunk, 0)
    y = _layer_norm(y_scr[...], lg_ref[...], lb_ref[...])
    o_ref[0] = (y * jax.nn.sigmoid(y)).astype(BF16)


def _conv(u, conv_w, conv_b, ln_g, ln_b, *, s, c, ts):
    bsz = u.shape[0]
    assert PAD_ROWS % ts == 0 and s % ts == 0 and ts % HALO_ROWS == 0 and ts % CONV_ROW_CHUNK == 0
    hb = ts // HALO_ROWS
    pb = PAD_ROWS // ts
    halo = lambda col: pl.BlockSpec((1, HALO_ROWS, c), lambda bi, i: (bi, (pb + i) * hb - 1, col))
    tile = lambda col: pl.BlockSpec((1, ts, c), lambda bi, i: (bi, pb + i, col))
    vec = lambda rows: pl.BlockSpec((rows, c), lambda bi, i: (0, 0))
    return pl.pallas_call(
        functools.partial(_conv_kernel, ts=ts, c=c),
        grid=(bsz, s // ts),
        in_specs=[halo(0), halo(1), tile(0), tile(1), vec(CONV_WIDTH), vec(1), vec(1), vec(1)],
        out_specs=pl.BlockSpec((1, ts, c), lambda bi, i: (bi, i, 0)),
        out_shape=jax.ShapeDtypeStruct((bsz, s, c), BF16),
        scratch_shapes=[pltpu.VMEM((ts + HALO_ROWS, c), F32), pltpu.VMEM((ts, c), F32)],
        compiler_params=_cparams(("arbitrary", "arbitrary")),
        name="conv",
    )(u, u, u, u, conv_w, conv_b, ln_g, ln_b)


def _attn_kernel(q_ref, k0_ref, k1_ref, k2_ref, v0_ref, v1_ref, v2_ref, bias_ref, o_ref, kw_scr, vw_scr, *, qb, aw):
    i = pl.program_id(1)
    win = 3 * qb
    for j, (kr, vr) in enumerate(((k0_ref, v0_ref), (k1_ref, v1_ref), (k2_ref, v2_ref))):
        kw_scr[j * qb:(j + 1) * qb, :] = kr[0]
        vw_scr[:, j * qb:(j + 1) * qb] = vr[0]
    lane = lax.broadcasted_iota(I32, (1, LANES), 1)
    first_head = lane < HEAD_DIM
    key_iota = lax.broadcasted_iota(I32, (BAND, LANES), 0)
    scale = jnp.asarray(HEAD_DIM ** -0.5, BF16)
    zero = jnp.zeros((), BF16)

    def run(masked):
        def head_pair(hp, carry):
            lo = pl.multiple_of(hp * LANES, LANES)
            qp = q_ref[0, :, pl.ds(lo, LANES)] * scale
            kp = kw_scr[:, pl.ds(lo, LANES)]
            vtp = vw_scr[pl.ds(lo, LANES), :]
            bias = bias_ref[hp]
            for ca in range(qb // CHUNK):
                rows = slice(ca * CHUNK, (ca + 1) * CHUNK)
                qc = qp[rows]
                qs = jnp.concatenate([jnp.where(first_head, qc, zero), jnp.where(first_head, zero, qc)], axis=0)
                st = lax.dot_general(kp[ca * CHUNK:ca * CHUNK + BAND], qs, (((1,), (1,)), ((), ())),
                                     preferred_element_type=F32) + bias
                if masked:
                    st = jnp.where(key_iota >= PAD_ROWS - (i * qb + ca * CHUNK), st, NEG_BIG)
                m = jnp.max(st, axis=0, keepdims=True)
                e = jnp.exp(st - m)
                l = jnp.sum(e, axis=0, keepdims=True)
                pieces = [e.astype(BF16)]
                if ca > 0:
                    pieces.insert(0, jnp.zeros((ca * CHUNK, LANES), BF16))
                if win - BAND - ca * CHUNK > 0:
                    pieces.append(jnp.zeros((win - BAND - ca * CHUNK, LANES), BF16))
                p = jnp.concatenate(pieces, axis=0)
                ot = jnp.dot(vtp, p, preferred_element_type=F32) * (1.0 / l)
                t = ot.T
                o_ref[0, rows, pl.ds(lo, LANES)] = jnp.where(first_head, t[0:CHUNK], t[CHUNK:]).astype(BF16)
            return carry

        lax.fori_loop(0, aw // LANES, head_pair, 0)

    first_valid_block = PAD_ROWS // qb

    @pl.when(i < first_valid_block)
    def _():
        run(True)

    @pl.when(i >= first_valid_block)
    def _():
        run(False)


def _attention(u, vt, bias_t, *, s, aw, qb, qcol):
    bsz = u.shape[0]
    assert PAD_ROWS == 2 * qb and s % qb == 0 and aw % LANES == 0
    pb = PAD_ROWS // qb
    kspec = lambda j: pl.BlockSpec((1, qb, aw), lambda bi, i: (bi, i + j, qcol + 1))
    vspec = lambda j: pl.BlockSpec((1, aw, qb), lambda bi, i: (bi, 0, i + j))
    return pl.pallas_call(
        functools.partial(_attn_kernel, qb=qb, aw=aw),
        grid=(bsz, s // qb),
        in_specs=[pl.BlockSpec((1, qb, aw), lambda bi, i: (bi, pb + i, qcol))]
        + [kspec(j) for j in range(3)] + [vspec(j) for j in range(3)]
        + [pl.BlockSpec(bias_t.shape, lambda bi, i: (0, 0, 0))],
        out_specs=pl.BlockSpec((1, qb, aw), lambda bi, i: (bi, i, 0)),
        out_shape=jax.ShapeDtypeStruct((bsz, s, aw), BF16),
        scratch_shapes=[pltpu.VMEM((3 * qb, aw), BF16), pltpu.VMEM((aw, 3 * qb), BF16)],
        compiler_params=_cparams(("arbitrary", "arbitrary")),
        name="attn",
    )(u, u, u, u, vt, vt, vt, bias_t)


def _expand_rel_bias(rel_bias):
    tail = jnp.broadcast_to(rel_bias[:, -1:], (rel_bias.shape[0], PAD_ROWS + CHUNK - 1 - MAX_REL))
    rev = jnp.concatenate([rel_bias, tail], axis=1)[:, ::-1].astype(F32)
    bias = jnp.stack([rev[:, CHUNK - 1 - i:CHUNK - 1 - i + BAND] for i in range(CHUNK)], axis=1)
    heads = bias.shape[0]
    return bias.reshape(heads // 2, 2, CHUNK, BAND).transpose(0, 3, 1, 2).reshape(heads // 2, BAND, 2 * CHUNK)


def _rows_to_tiles(h, ref, *, rows, d):
    for ci in range(d // LANES):
        ref[pl.ds(ci, rows, stride=d // LANES), :] = h[:, ci * LANES:(ci + 1) * LANES]


def _tiles_to_rows(ref, x_scr, *, rows, d):
    for ci in range(d // LANES):
        x_scr[:, ci * LANES:(ci + 1) * LANES] = ref[pl.ds(ci, rows, stride=d // LANES), :].astype(x_scr.dtype)


def _outproj_kernel(conv_ref, att_ref, h_ref, wo_ref, g_ref, b_ref, wr_ref, wsg_ref, wsu_ref, wsd_ref,
                    base_ref, h1r_ref, lg_ref, *, alpha, c, tm, d):
    mixed = jnp.dot(conv_ref[...], wo_ref[0:c, :], preferred_element_type=F32)
    mixed = mixed + jnp.dot(att_ref[...], wo_ref[c:, :], preferred_element_type=F32)
    h1 = _layer_norm(alpha * h_ref[...] + mixed, g_ref[...], b_ref[...])
    h1b = h1.astype(BF16)
    lg_ref[...] = lax.dot_general(wr_ref[...], h1b, (((1,), (1,)), ((), ())), preferred_element_type=F32)
    sg = jnp.dot(h1b, wsg_ref[...], preferred_element_type=F32)
    su = jnp.dot(h1b, wsu_ref[...], preferred_element_type=F32)
    hid = (sg * jax.nn.sigmoid(sg) * su).astype(BF16)
    base_ref[...] = alpha * h1 + jnp.dot(hid, wsd_ref[...], preferred_element_type=F32)
    _rows_to_tiles(h1, h1r_ref, rows=tm, d=d)


def _outproj(conv_out, att_out, h, wo, g, b, wr_t, wsg, wsu, wsd, *, alpha, tm):
    n, d = h.shape
    c = conv_out.shape[1]
    fs = wsg.shape[1]
    rpt = d // LANES
    const = lambda shape: pl.BlockSpec(shape, lambda i: (0,) * len(shape), pipeline_mode=pl.Buffered(1))
    return pl.pallas_call(
        functools.partial(_outproj_kernel, alpha=alpha, c=c, tm=tm, d=d),
        grid=(n // tm,),
        in_specs=[
            pl.BlockSpec((tm, c), lambda i: (i, 0)),
            pl.BlockSpec((tm, att_out.shape[1]), lambda i: (i, 0)),
            pl.BlockSpec((tm, d), lambda i: (i, 0)),
            const(wo.shape), const((1, d)), const((1, d)), const(wr_t.shape),
            const((d, fs)), const((d, fs)), const((fs, d)),
        ],
        out_specs=[
            pl.BlockSpec((tm, d), lambda i: (i, 0)),
            pl.BlockSpec((tm * rpt, LANES), lambda i: (i, 0)),
            pl.BlockSpec((N_EXPERTS, tm), lambda i: (0, i)),
        ],
        out_shape=[
            jax.ShapeDtypeStruct((n, d), F32),
            jax.ShapeDtypeStruct((n * rpt, LANES), F32),
            jax.ShapeDtypeStruct((N_EXPERTS, n), F32),
        ],
        compiler_params=_cparams(("arbitrary",)),
        name="outproj",
    )(conv_out, att_out, h, wo, g, b, wr_t, wsg, wsu, wsd)


def _router_kernel(lg_ref, rb_ref, tri_ref, pos_ref, gd_ref, cnt_ref, carry_scr, *, tt):
    i = pl.program_id(0)

    @pl.when(i == 0)
    def _():
        carry_scr[...] = jnp.zeros_like(carry_scr)

    scores = jax.nn.sigmoid(lg_ref[...])
    sel = scores + rb_ref[...]
    sel3 = sel.reshape(GROUP_SIZE, N_GROUPS, tt)
    jio = lax.broadcasted_iota(I32, (GROUP_SIZE, N_GROUPS, tt), 0)
    m1 = jnp.max(sel3, axis=0)
    first = jnp.min(jnp.where(sel3 == m1[None], jio, GROUP_SIZE), axis=0)
    m2 = jnp.max(jnp.where(jio == first[None], -jnp.inf, sel3), axis=0)
    gs = m1 + m2
    gio = lax.broadcasted_iota(I32, (N_GROUPS, tt), 0)
    grank = jnp.zeros((N_GROUPS, tt), I32)
    for gp in range(N_GROUPS):
        row = gs[gp:gp + 1, :]
        tie = jnp.where(gio > gp, 1, 0)
        grank = grank + jnp.where(row > gs, 1, jnp.where(row == gs, tie, 0))
    gmask = grank < TOPK_GROUPS
    masked = jnp.where(gmask[None], sel3, -jnp.inf).reshape(N_EXPERTS, tt)
    rio = lax.broadcasted_iota(I32, (N_EXPERTS, tt), 0)
    eid = (rio % N_GROUPS) * GROUP_SIZE + rio // N_GROUPS
    erank = jnp.zeros((N_EXPERTS, tt), I32)
    for rp in range(N_EXPERTS):
        ep = (rp % N_GROUPS) * GROUP_SIZE + rp // N_GROUPS
        row = masked[rp:rp + 1, :]
        tie = jnp.where(eid > ep, 1, 0)
        erank = erank + jnp.where(row > masked, 1, jnp.where(row == masked, tie, 0))
    smask = erank < TOP_K
    picked = jnp.where(smask, scores, 0.0)
    wsum = jnp.sum(picked, axis=0, keepdims=True)
    gd_ref[...] = picked / wsum * ROUTED_SCALE
    ones = jnp.where(smask, 1.0, 0.0)
    incl = jnp.dot(ones.astype(BF16), tri_ref[...], preferred_element_type=F32)
    pos = carry_scr[...] + incl - ones
    pos_ref[...] = jnp.where(smask, pos, -1.0).astype(I32)
    carry = carry_scr[...] + jnp.sum(ones, axis=1, keepdims=True)
    carry_scr[...] = carry
    cnt_ref[...] = jnp.broadcast_to(carry, (N_EXPERTS, LANES)).astype(I32)


def _router(logits_t, rb, tri, *, tt):
    n = logits_t.shape[1]
    tile = pl.BlockSpec((N_EXPERTS, tt), lambda i: (0, i))
    return pl.pallas_call(
        functools.partial(_router_kernel, tt=tt),
        grid=(n // tt,),
        in_specs=[tile, pl.BlockSpec((N_EXPERTS, 1), lambda i: (0, 0)), pl.BlockSpec((tt, tt), lambda i: (0, 0))],
        out_specs=[tile, tile, pl.BlockSpec((N_EXPERTS, LANES), lambda i: (0, 0))],
        out_shape=[
            jax.ShapeDtypeStruct((N_EXPERTS, n), I32),
            jax.ShapeDtypeStruct((N_EXPERTS, n), F32),
            jax.ShapeDtypeStruct((N_EXPERTS, LANES), I32),
        ],
        scratch_shapes=[pltpu.VMEM((N_EXPERTS, 1), F32)],
        compiler_params=_cparams(("arbitrary",)),
        name="router",
    )(logits_t, rb, tri)


def _slots_kernel(pos_ref, gd_ref, ps_ref, ltri_ref, dest_ref, gate_ref):
    pos = pos_ref[...]
    gd = gd_ref[...]
    chosen = pos >= 0
    ones = jnp.where(chosen, 1.0, 0.0).astype(BF16)
    before = jnp.dot(ltri_ref[...], ones, preferred_element_type=F32)
    dest_full = pos + ps_ref[...]
    for k in range(TOP_K):
        hit = jnp.where(chosen, before, -1.0) == float(k)
        dest_ref[k:k + 1, :] = jnp.sum(jnp.where(hit, dest_full, 0), axis=0, keepdims=True)
        gate_ref[k:k + 1, :] = jnp.sum(jnp.where(hit, gd, 0.0), axis=0, keepdims=True)


def _slots(pos, gd, pad_start, ltri, *, tt):
    n = pos.shape[1]
    tile = pl.BlockSpec((N_EXPERTS, tt), lambda i: (0, i))
    otile = pl.BlockSpec((TOP_K, tt), lambda i: (0, i))
    return pl.pallas_call(
        _slots_kernel,
        grid=(n // tt,),
        in_specs=[tile, tile, pl.BlockSpec((N_EXPERTS, 1), lambda i: (0, 0)),
                  pl.BlockSpec((N_EXPERTS, N_EXPERTS), lambda i: (0, 0))],
        out_specs=[otile, otile],
        out_shape=[jax.ShapeDtypeStruct((TOP_K, n), I32), jax.ShapeDtypeStruct((TOP_K, n), F32)],
        compiler_params=_cparams(("arbitrary",)),
        name="slots",
    )(pos, gd, pad_start, ltri)


def _dispatch_kernel(pad_end_ref, cnt_ref, dest_ref, rows_ref, xs_ref, zero_scr, zsem, sem, *, td, rpt, rb):
    i = pl.program_id(0)

    def zero_copy(r):
        row0 = pl.multiple_of((pad_end_ref[r] - rb) * rpt, SUBLANES)
        return pltpu.make_async_copy(zero_scr, xs_ref.at[pl.ds(row0, rb * rpt)], zsem)

    @pl.when(i == 0)
    def _():
        zero_scr[...] = jnp.zeros_like(zero_scr)

        def start(r, carry):
            @pl.when(cnt_ref[r] > 0)
            def _():
                zero_copy(r).start()
            return carry

        def wait(r, carry):
            @pl.when(cnt_ref[r] > 0)
            def _():
                zero_copy(r).wait()
            return carry

        lax.fori_loop(0, N_EXPERTS, start, 0)
        lax.fori_loop(0, N_EXPERTS, wait, 0)

    def token(t, carry):
        src = rows_ref.at[pl.ds(pl.multiple_of(t * rpt, SUBLANES), rpt)]
        for k in range(TOP_K):
            row0 = pl.multiple_of(dest_ref[t * TOP_K + k] * rpt, SUBLANES)
            pltpu.make_async_copy(src, xs_ref.at[pl.ds(row0, rpt)], sem).start()
        return carry

    lax.fori_loop(0, td, token, 0)
    for k in range(TOP_K):
        pltpu.make_async_copy(rows_ref, xs_ref.at[pl.ds(0, td * rpt)], sem).wait()


def _dispatch(pad_end, counts, dest_flat, h1r, *, n_rows, td, rpt, rb):
    n = h1r.shape[0] // rpt
    grid_spec = pltpu.PrefetchScalarGridSpec(
        num_scalar_prefetch=2,
        grid=(n // td,),
        in_specs=[
            pl.BlockSpec((td * TOP_K,), lambda i, pe, cn: (i,), memory_space=pltpu.SMEM),
            pl.BlockSpec((td * rpt, LANES), lambda i, pe, cn: (i, 0)),
        ],
        out_specs=pl.BlockSpec(memory_space=pl.ANY),
        scratch_shapes=[pltpu.VMEM((rb * rpt, LANES), F32), pltpu.SemaphoreType.DMA(()), pltpu.SemaphoreType.DMA(())],
    )
    return pl.pallas_call(
        functools.partial(_dispatch_kernel, td=td, rpt=rpt, rb=rb),
        grid_spec=grid_spec,
        out_shape=jax.ShapeDtypeStruct((n_rows * rpt, LANES), F32),
        compiler_params=_cparams(("arbitrary",)),
        name="dispatch",
    )(pad_end, counts, dest_flat, h1r)


def _expert_kernel(bidx_ref, bexp_ref, nused_ref, xs_ref, wg_ref, wu_ref, wd_ref, y_ref,
                   x_scr, wg_scr, wu_scr, wd_scr, *, rb, d):
    b = pl.program_id(0)
    new_expert = (b == 0) | (bexp_ref[b] != bexp_ref[jnp.maximum(b - 1, 0)])

    @pl.when(new_expert)
    def _():
        wg_scr[...] = wg_ref[0].astype(BF16)
        wu_scr[...] = wu_ref[0].astype(BF16)
        wd_scr[...] = wd_ref[0].astype(BF16)

    @pl.when(b < nused_ref[0])
    def _():
        _tiles_to_rows(xs_ref, x_scr, rows=rb, d=d)
        x = x_scr[...]
        g = jnp.dot(x, wg_scr[...], preferred_element_type=F32)
        u = jnp.dot(x, wu_scr[...], preferred_element_type=F32)
        hid = (g * jax.nn.sigmoid(g) * u).astype(BF16)
        y = jnp.dot(hid, wd_scr[...], preferred_element_type=F32)
        _rows_to_tiles(y, y_ref, rows=rb, d=d)


def _experts(bidx, bexp, nused, xs, wg, wu, wd, *, rb, d):
    ypt = d // LANES
    n_blocks = xs.shape[0] // (rb * ypt)
    f = wg.shape[2]
    grid_spec = pltpu.PrefetchScalarGridSpec(
        num_scalar_prefetch=3,
        grid=(n_blocks,),
        in_specs=[
            pl.BlockSpec((rb * ypt, LANES), lambda b, bi, be, nu: (bi[b], 0)),
            pl.BlockSpec((1, d, f), lambda b, bi, be, nu: (be[b], 0, 0)),
            pl.BlockSpec((1, d, f), lambda b, bi, be, nu: (be[b], 0, 0)),
            pl.BlockSpec((1, f, d), lambda b, bi, be, nu: (be[b], 0, 0)),
        ],
        out_specs=pl.BlockSpec((rb * ypt, LANES), lambda b, bi, be, nu: (bi[b], 0)),
        scratch_shapes=[pltpu.VMEM((rb, d), BF16), pltpu.VMEM((d, f), BF16), pltpu.VMEM((d, f), BF16),
                        pltpu.VMEM((f, d), BF16)],
    )
    return pl.pallas_call(
        functools.partial(_expert_kernel, rb=rb, d=d),
        grid_spec=grid_spec,
        out_shape=jax.ShapeDtypeStruct((n_blocks * rb * ypt, LANES), F32),
        compiler_params=_cparams(("arbitrary",)),
        name="experts",
    )(bidx, bexp, nused, xs, wg, wu, wd)


def _combine_kernel(dest_ref, gate_ref, base_ref, g_ref, b_ref, y_ref, o_ref, buf, r_scr, sem, *, tc, ypt, nt):
    i = pl.program_id(0)
    slot = i % 2

    @pl.when(i < nt)
    def _():
        def token(t, carry):
            for k in range(TOP_K):
                row0 = pl.multiple_of(dest_ref[t * TOP_K + k] * ypt, ypt)
                dst = buf.at[slot * TOP_K + k, pl.ds(pl.multiple_of(t * ypt, ypt), ypt)]
                pltpu.make_async_copy(y_ref.at[pl.ds(row0, ypt)], dst, sem.at[slot]).start()
            return carry

        lax.fori_loop(0, tc, token, 0)

    @pl.when(i > 0)
    def _():
        ps = 1 - slot
        for k in range(TOP_K):
            pltpu.make_async_copy(y_ref.at[pl.ds(0, tc * ypt)], buf.at[ps * TOP_K + k], sem.at[ps]).wait()
        gate = gate_ref[...]
        gk = [jnp.broadcast_to(gate[:, k:k + 1], (tc, LANES)) for k in range(TOP_K)]
        for ci in range(ypt):
            lanes = slice(ci * LANES, (ci + 1) * LANES)
            acc = base_ref[:, lanes]
            for k in range(TOP_K):
                acc = acc + gk[k] * buf[ps * TOP_K + k, pl.ds(ci, tc, stride=ypt), :]
            r_scr[:, lanes] = acc
        o_ref[...] = _layer_norm(r_scr[...], g_ref[...], b_ref[...])


def _combine(dest_flat, gate_t, base, g, b, y, *, tc):
    n, d = base.shape
    ypt = d // LANES
    nt = n // tc
    prev = lambda i: (jnp.maximum(i - 1, 0), 0)
    return pl.pallas_call(
        functools.partial(_combine_kernel, tc=tc, ypt=ypt, nt=nt),
        grid=(nt + 1,),
        in_specs=[
            pl.BlockSpec((tc * TOP_K,), lambda i: (jnp.minimum(i, nt - 1),), memory_space=pltpu.SMEM),
            pl.BlockSpec((tc, TOP_K), prev),
            pl.BlockSpec((tc, d), prev),
            pl.BlockSpec((1, d), lambda i: (0, 0)),
            pl.BlockSpec((1, d), lambda i: (0, 0)),
            pl.BlockSpec(memory_space=pl.ANY),
        ],
        out_specs=pl.BlockSpec((tc, d), prev),
        out_shape=jax.ShapeDtypeStruct((n, d), F32),
        scratch_shapes=[
            pltpu.VMEM((2 * TOP_K, tc * ypt, LANES), F32),
            pltpu.VMEM((tc, d), F32),
            pltpu.SemaphoreType.DMA((2,)),
        ],
        compiler_params=_cparams(("arbitrary",)),
        name="combine",
    )(dest_flat, gate_t, base, g, b, y)


def _row_vec(v):
    return v.reshape(1, -1).astype(F32)


def _expert_rows(a):
    return a.reshape((N_GROUPS, GROUP_SIZE) + a.shape[1:]).swapaxes(0, 1).reshape(a.shape)


def _tile(n, want):
    t = min(n, want)
    assert n % t == 0
    return t


def kernel(x, ln_in_g, ln_in_b, w_in, conv_w, conv_b, conv_ln_g, conv_ln_b, rel_bias, w_out, ln1_g, ln1_b,
           w_router, router_bias, w_gate, w_up, w_down, w_shared_gate, w_shared_up, w_shared_down, ln2_g, ln2_b):
    bsz, s, d = x.shape
    n = bsz * s
    depth = w_in.shape[0]
    c = conv_w.shape[2]
    aw = rel_bias.shape[1] * HEAD_DIM
    assert c == aw and w_in.shape[2] == 2 * c + 3 * aw and w_router.shape[2] == N_EXPERTS
    alpha = (2 * depth) ** 0.25
    rpt = d // LANES
    rb = EXPERT_ROW_BLOCK
    n_blocks = -(-n * TOP_K // rb) + N_EXPERTS
    n_rows = n_blocks * rb
    tt = _tile(n, 1024)
    tri = jnp.triu(jnp.ones((tt, tt), BF16))
    ltri = jnp.tril(jnp.ones((N_EXPERTS, N_EXPERTS), BF16), -1)

    h = x
    for li in range(depth):
        hn, u, vt = _inproj(h, _row_vec(ln_in_g), _row_vec(ln_in_b), w_in[li, :, :2 * c + 2 * aw].astype(BF16),
                            w_in[li, :, 2 * c + 2 * aw:].T.astype(BF16), apply_ln=(li == 0), tm=_tile(s, 512), tn=c)
        conv_out = _conv(u, conv_w[li].astype(F32), _row_vec(conv_b[li]), _row_vec(conv_ln_g[li]),
                         _row_vec(conv_ln_b[li]), s=s, c=c, ts=_tile(s, 512))
        att_out = _attention(u, vt, _expand_rel_bias(rel_bias[li]), s=s, aw=aw, qb=PAD_ROWS // 2, qcol=2)
        wr_t = _expert_rows(w_router[li].T).astype(BF16)
        base, h1r, logits_t = _outproj(
            conv_out.reshape(n, c), att_out.reshape(n, aw), hn.reshape(n, d), w_out[li].astype(BF16),
            _row_vec(ln1_g[li]), _row_vec(ln1_b[li]), wr_t, w_shared_gate[li].astype(BF16),
            w_shared_up[li].astype(BF16), w_shared_down[li].astype(BF16), alpha=alpha, tm=_tile(n, 512))
        rbias = _expert_rows(router_bias[li].reshape(N_EXPERTS, 1)).astype(F32)
        pos, gd, cnt = _router(logits_t, rbias, tri, tt=tt)
        counts = cnt[:, 0]
        padded = (counts + rb - 1) // rb * rb
        pad_end = jnp.cumsum(padded).astype(I32)
        pad_start = pad_end - padded
        dest, gate = _slots(pos, gd, pad_start.reshape(N_EXPERTS, 1), ltri, tt=tt)
        dest_flat = dest.T.reshape(n * TOP_K)
        xs = _dispatch(pad_end, counts, dest_flat, h1r, n_rows=n_rows, td=_tile(n, 256), rpt=rpt, rb=rb)
        n_used = pad_end[-1] // rb
        blk = jnp.minimum(jnp.arange(n_blocks, dtype=I32), n_used - 1)
        blk_row = jnp.sum((pad_end[None, :] <= (blk * rb)[:, None]).astype(I32), axis=1)
        blk_exp = (blk_row % N_GROUPS) * GROUP_SIZE + blk_row // N_GROUPS
        y = _experts(blk, blk_exp, n_used.reshape(1), xs, w_gate[li], w_up[li], w_down[li], rb=rb, d=d)
        h = _combine(dest_flat, gate.T, base, _row_vec(ln2_g[li]), _row_vec(ln2_b[li]), y,
                     tc=_tile(n, 128)).reshape(bsz, s, d)
    return h
```

```python
import functools

import jax
import jax.numpy as jnp
from jax import lax
from jax.experimental import pallas as pl
from jax.experimental.pallas import tpu as pltpu

CHUNK = 64
CONV_WIDTH = 31
HEAD_DIM = 64
LEFT_CHUNKS = 8
BAND = (LEFT_CHUNKS + 1) * CHUNK
MAX_REL = 256
N_EXPERTS = 64
TOP_K = 8
N_GROUPS = 8
GROUP_SIZE = N_EXPERTS // N_GROUPS
TOPK_GROUPS = 4
ROUTED_SCALE = 2.5
LN_EPS = 1e-5

LANES = 128
SUBLANES = 8
VMEM_LIMIT = 56 * 1024 * 1024

PAD_ROWS = LEFT_CHUNKS * CHUNK
HALO_ROWS = 32
CONV_ROW_CHUNK = 64
EXPERT_ROW_BLOCK = 256
NEG_BIG = -1e30

F32 = jnp.float32
BF16 = jnp.bfloat16
I32 = jnp.int32


def _cparams(semantics, vmem=VMEM_LIMIT):
    return pltpu.CompilerParams(dimension_semantics=semantics, vmem_limit_bytes=vmem)


def _layer_norm(x, g, b):
    mu = jnp.mean(x, axis=-1, keepdims=True)
    xc = x - mu
    var = jnp.mean(xc * xc, axis=-1, keepdims=True)
    return xc * lax.rsqrt(var + LN_EPS) * g + b


def _inproj_kernel(x_ref, g_ref, b_ref, w_ref, wvt_ref, h_ref, u_ref, vt_ref, hn_ref, *, apply_ln, n_u):
    i = pl.program_id(1)
    j = pl.program_id(2)

    @pl.when(i == 0)
    def _():
        u_ref[...] = jnp.zeros_like(u_ref)
        vt_ref[...] = jnp.zeros_like(vt_ref)

    @pl.when((i > 0) & (j == 0))
    def _():
        x = x_ref[0]
        h = _layer_norm(x, g_ref[...], b_ref[...]) if apply_ln else x
        h_ref[0] = h
        hn_ref[...] = h.astype(BF16)

    @pl.when((i > 0) & (j < n_u))
    def _():
        u_ref[0] = jnp.dot(hn_ref[...], w_ref[...], preferred_element_type=F32).astype(BF16)

    @pl.when((i > 0) & (j == n_u))
    def _():
        vt = lax.dot_general(wvt_ref[...], hn_ref[...], (((1,), (1,)), ((), ())), preferred_element_type=F32)
        vt_ref[0] = vt.astype(BF16)


def _inproj(x, g, b, w_u, w_vt, *, apply_ln, tm, tn):
    bsz, s, d = x.shape
    width = w_u.shape[1]
    aw = w_vt.shape[0]
    assert PAD_ROWS % tm == 0 and s % tm == 0 and width % tn == 0
    pad_blocks = PAD_ROWS // tm
    n_u = width // tn
    grid = (bsz, s // tm + pad_blocks, n_u + 1)
    row = lambda bi, i, j: (bi, jnp.maximum(i - pad_blocks, 0), 0)
    const = lambda shape: pl.BlockSpec(shape, lambda bi, i, j: (0,) * len(shape), pipeline_mode=pl.Buffered(1))
    return pl.pallas_call(
        functools.partial(_inproj_kernel, apply_ln=apply_ln, n_u=n_u),
        grid=grid,
        in_specs=[
            pl.BlockSpec((1, tm, d), row),
            const((1, d)), const((1, d)),
            pl.BlockSpec((d, tn), lambda bi, i, j: (0, jnp.minimum(j, n_u - 1))),
            const((aw, d)),
        ],
        out_specs=[
            pl.BlockSpec((1, tm, d), row),
            pl.BlockSpec((1, tm, tn), lambda bi, i, j: (bi, i, jnp.minimum(j, n_u - 1))),
            pl.BlockSpec((1, aw, tm), lambda bi, i, j: (bi, 0, i)),
        ],
        out_shape=[
            jax.ShapeDtypeStruct((bsz, s, d), F32),
            jax.ShapeDtypeStruct((bsz, s + PAD_ROWS, width), BF16),
            jax.ShapeDtypeStruct((bsz, aw, s + PAD_ROWS), BF16),
        ],
        scratch_shapes=[pltpu.VMEM((tm, d), BF16)],
        compiler_params=_cparams(("arbitrary", "arbitrary", "arbitrary")),
        name="inproj",
    )(x, g, b, w_u, w_vt)


def _conv_kernel(ah_ref, gh_ref, at_ref, gt_ref, w_ref, cb_ref, lg_ref, lb_ref, o_ref, v_scr, y_scr, *, ts, c):
    v_scr[0:HALO_ROWS, :] = ah_ref[0].astype(F32) * jax.nn.sigmoid(gh_ref[0].astype(F32))
    v_scr[HALO_ROWS:HALO_ROWS + ts, :] = at_ref[0].astype(F32) * jax.nn.sigmoid(gt_ref[0].astype(F32))
    first_tap = HALO_ROWS - (CONV_WIDTH - 1)
    win_rows = CONV_ROW_CHUNK + HALO_ROWS

    def row_chunk(rc, carry):
        r0 = pl.multiple_of(rc * CONV_ROW_CHUNK, CONV_ROW_CHUNK)
        for ci in range(c // LANES):
            lanes = slice(ci * LANES, (ci + 1) * LANES)
            win = v_scr[pl.ds(r0, win_rows), lanes]
            shifted = [win] + [pltpu.roll(win, win_rows - sh, 0) for sh in range(1, SUBLANES)]
            wc = w_ref[:, lanes]
            acc = jnp.zeros((CONV_ROW_CHUNK, LANES), F32)
            for k in range(CONV_WIDTH):
                start = (first_tap + k) // SUBLANES * SUBLANES
                acc = acc + wc[k:k + 1, :] * shifted[(first_tap + k) % SUBLANES][start:start + CONV_ROW_CHUNK, :]
            y_scr[pl.ds(r0, CONV_ROW_CHUNK), lanes] = acc + cb_ref[:, lanes]
        return carry

    lax.fori_loop(0, ts // CONV_ROW_CHUNK, row_chunk, 0)
    y = _layer_norm(y_scr[...], lg_ref[...], lb_ref[...])
    o_ref[0] = (y * jax.nn.sigmoid(y)).astype(BF16)


def _conv(u, conv_w, conv_b, ln_g, ln_b, *, s, c, ts):
    bsz = u.shape[0]
    assert PAD_ROWS % ts == 0 and s % ts == 0 and ts % HALO_ROWS == 0 and ts % CONV_ROW_CHUNK == 0
    hb = ts // HALO_ROWS
    pb = PAD_ROWS // ts
    halo = lambda col: pl.BlockSpec((1, HALO_ROWS, c), lambda bi, i: (bi, (pb + i) * hb - 1, col))
    tile = lambda col: pl.BlockSpec((1, ts, c), lambda bi, i: (bi, pb + i, col))
    vec = lambda rows: pl.BlockSpec((rows, c), lambda bi, i: (0, 0))
    return pl.pallas_call(
        functools.partial(_conv_kernel, ts=ts, c=c),
        grid=(bsz, s // ts),
        in_specs=[halo(0), halo(1), tile(0), tile(1), vec(CONV_WIDTH), vec(1), vec(1), vec(1)],
        out_specs=pl.BlockSpec((1, ts, c), lambda bi, i: (bi, i, 0)),
        out_shape=jax.ShapeDtypeStruct((bsz, s, c), BF16),
        scratch_shapes=[pltpu.VMEM((ts + HALO_ROWS, c), F32), pltpu.VMEM((ts, c), F32)],
        compiler_params=_cparams(("arbitrary", "arbitrary")),
        name="conv",
    )(u, u, u, u, conv_w, conv_b, ln_g, ln_b)


def _attn_kernel(q_ref, k0_ref, k1_ref, k2_ref, v0_ref, v1_ref, v2_ref, bias_ref, o_ref, kw_scr, vw_scr, st_scr, p_scr,
                 *, qb, aw):
    i = pl.program_id(1)
    for j, (kr, vr) in enumerate(((k0_ref, v0_ref), (k1_ref, v1_ref), (k2_ref, v2_ref))):
        kw_scr[j * qb:(j + 1) * qb, :] = kr[0]
        vw_scr[:, j * qb:(j + 1) * qb] = vr[0]
    lane = lax.broadcasted_iota(I32, (1, LANES), 1)
    first_head = lane < HEAD_DIM
    key_iota = lax.broadcasted_iota(I32, (BAND, LANES), 0)
    scale = jnp.asarray(HEAD_DIM ** -0.5, BF16)
    zero = jnp.zeros((), BF16)

    @pl.when((pl.program_id(0) == 0) & (i == 0))
    def _():
        p_scr[...] = jnp.zeros_like(p_scr)

    def run(masked):
        def head_pair(hp, carry):
            lo = pl.multiple_of(hp * LANES, LANES)
            qp = q_ref[0, :, pl.ds(lo, LANES)] * scale
            vtp = vw_scr[pl.ds(lo, LANES), :]
            bias = bias_ref[hp]
            n_chunks = qb // CHUNK
            for ca in range(n_chunks):
                qc = qp[ca * CHUNK:(ca + 1) * CHUNK]
                qs = jnp.concatenate([jnp.where(first_head, qc, zero), jnp.where(first_head, zero, qc)], axis=0)
                st = lax.dot_general(kw_scr[ca * CHUNK:ca * CHUNK + BAND, pl.ds(lo, LANES)], qs,
                                     (((1,), (1,)), ((), ())), preferred_element_type=F32) + bias
                if masked:
                    st = jnp.where(key_iota >= PAD_ROWS - (i * qb + ca * CHUNK), st, NEG_BIG)
                st_scr[ca] = st
            inv_l = []
            for ca in range(n_chunks):
                st = st_scr[ca]
                m = jnp.max(st, axis=0, keepdims=True)
                e = jnp.exp(st - m)
                inv_l.append(1.0 / jnp.sum(e, axis=0, keepdims=True))
                p_scr[ca, ca * CHUNK:ca * CHUNK + BAND, :] = e.astype(BF16)
            for ca in range(n_chunks):
                ot = jnp.dot(vtp, p_scr[ca], preferred_element_type=F32) * inv_l[ca]
                t = ot.T
                o_ref[0, ca * CHUNK:(ca + 1) * CHUNK, pl.ds(lo, LANES)] = jnp.where(
                    first_head, t[0:CHUNK], t[CHUNK:]).astype(BF16)
            return carry

        lax.fori_loop(0, aw // LANES, head_pair, 0)

    first_valid_block = PAD_ROWS // qb

    @pl.when(i < first_valid_block)
    def _():
        run(True)

    @pl.when(i >= first_valid_block)
    def _():
        run(False)


def _attention(u, vt, bias_t, *, s, aw, qb, qcol):
    bsz = u.shape[0]
    assert PAD_ROWS == 2 * qb and s % qb == 0 and aw % LANES == 0
    pb = PAD_ROWS // qb
    kspec = lambda j: pl.BlockSpec((1, qb, aw), lambda bi, i: (bi, i + j, qcol + 1))
    vspec = lambda j: pl.BlockSpec((1, aw, qb), lambda bi, i: (bi, 0, i + j))
    return pl.pallas_call(
        functools.partial(_attn_kernel, qb=qb, aw=aw),
        grid=(bsz, s // qb),
        in_specs=[pl.BlockSpec((1, qb, aw), lambda bi, i: (bi, pb + i, qcol))]
        + [kspec(j) for j in range(3)] + [vspec(j) for j in range(3)]
        + [pl.BlockSpec(bias_t.shape, lambda bi, i: (0, 0, 0))],
        out_specs=pl.BlockSpec((1, qb, aw), lambda bi, i: (bi, i, 0)),
        out_shape=jax.ShapeDtypeStruct((bsz, s, aw), BF16),
        scratch_shapes=[pltpu.VMEM((3 * qb, aw), BF16), pltpu.VMEM((aw, 3 * qb), BF16),
                        pltpu.VMEM((qb // CHUNK, BAND, LANES), F32), pltpu.VMEM((qb // CHUNK, 3 * qb, LANES), BF16)],
        compiler_params=_cparams(("arbitrary", "arbitrary")),
        name="attn",
    )(u, u, u, u, vt, vt, vt, bias_t)


def _expand_rel_bias(rel_bias):
    tail = jnp.broadcast_to(rel_bias[:, -1:], (rel_bias.shape[0], PAD_ROWS + CHUNK - 1 - MAX_REL))
    rev = jnp.concatenate([rel_bias, tail], axis=1)[:, ::-1].astype(F32)
    bias = jnp.stack([rev[:, CHUNK - 1 - i:CHUNK - 1 - i + BAND] for i in range(CHUNK)], axis=1)
    heads = bias.shape[0]
    return bias.reshape(heads // 2, 2, CHUNK, BAND).transpose(0, 3, 1, 2).reshape(heads // 2, BAND, 2 * CHUNK)


def _rows_to_tiles(h, ref, *, rows, d):
    for ci in range(d // LANES):
        ref[pl.ds(ci, rows, stride=d // LANES), :] = h[:, ci * LANES:(ci + 1) * LANES]


def _outproj_kernel(conv_ref, att_ref, h_ref, wo_ref, g_ref, b_ref, wr_ref, wsg_ref, wsu_ref, wsd_ref,
                    base_ref, h1r_ref, lg_ref, *, alpha, c, tm, d):
    mixed = jnp.dot(conv_ref[...], wo_ref[0:c, :], preferred_element_type=F32)
    mixed = mixed + jnp.dot(att_ref[...], wo_ref[c:, :], preferred_element_type=F32)
    h1 = _layer_norm(alpha * h_ref[...] + mixed, g_ref[...], b_ref[...])
    h1b = h1.astype(BF16)
    lg_ref[...] = lax.dot_general(wr_ref[...], h1b, (((1,), (1,)), ((), ())), preferred_element_type=F32)
    sg = jnp.dot(h1b, wsg_ref[...], preferred_element_type=F32)
    su = jnp.dot(h1b, wsu_ref[...], preferred_element_type=F32)
    hid = (sg * jax.nn.sigmoid(sg) * su).astype(BF16)
    base_ref[...] = alpha * h1 + jnp.dot(hid, wsd_ref[...], preferred_element_type=F32)
    _rows_to_tiles(h1, h1r_ref, rows=tm, d=d)


def _outproj(conv_out, att_out, h, wo, g, b, wr_t, wsg, wsu, wsd, *, alpha, tm):
    n, d = h.shape
    c = conv_out.shape[1]
    fs = wsg.shape[1]
    rpt = d // LANES
    const = lambda shape: pl.BlockSpec(shape, lambda i: (0,) * len(shape), pipeline_mode=pl.Buffered(1))
    return pl.pallas_call(
        functools.partial(_outproj_kernel, alpha=alpha, c=c, tm=tm, d=d),
        grid=(n // tm,),
        in_specs=[
            pl.BlockSpec((tm, c), lambda i: (i, 0)),
            pl.BlockSpec((tm, att_out.shape[1]), lambda i: (i, 0)),
            pl.BlockSpec((tm, d), lambda i: (i, 0)),
            const(wo.shape), const((1, d)), const((1, d)), const(wr_t.shape),
            const((d, fs)), const((d, fs)), const((fs, d)),
        ],
        out_specs=[
            pl.BlockSpec((tm, d), lambda i: (i, 0)),
            pl.BlockSpec((tm * rpt, LANES), lambda i: (i, 0)),
            pl.BlockSpec((N_EXPERTS, tm), lambda i: (0, i)),
        ],
        out_shape=[
            jax.ShapeDtypeStruct((n, d), F32),
            jax.ShapeDtypeStruct((n * rpt, LANES), F32),
            jax.ShapeDtypeStruct((N_EXPERTS, n), F32),
        ],
        compiler_params=_cparams(("arbitrary",)),
        name="outproj",
    )(conv_out, att_out, h, wo, g, b, wr_t, wsg, wsu, wsd)


def _router_kernel(lg_ref, rb_ref, tri_ref, pos_ref, gd_ref, cnt_ref, carry_scr, *, tt):
    i = pl.program_id(0)

    @pl.when(i == 0)
    def _():
        carry_scr[...] = jnp.zeros_like(carry_scr)

    scores = jax.nn.sigmoid(lg_ref[...])
    sel = scores + rb_ref[...]
    sel3 = sel.reshape(GROUP_SIZE, N_GROUPS, tt)
    jio = lax.broadcasted_iota(I32, (GROUP_SIZE, N_GROUPS, tt), 0)
    m1 = jnp.max(sel3, axis=0)
    first = jnp.min(jnp.where(sel3 == m1[None], jio, GROUP_SIZE), axis=0)
    m2 = jnp.max(jnp.where(jio == first[None], -jnp.inf, sel3), axis=0)
    gs = m1 + m2
    gio = lax.broadcasted_iota(I32, (N_GROUPS, tt), 0)
    grank = jnp.zeros((N_GROUPS, tt), I32)
    for gp in range(N_GROUPS):
        row = gs[gp:gp + 1, :]
        tie = jnp.where(gio > gp, 1, 0)
        grank = grank + jnp.where(row > gs, 1, jnp.where(row == gs, tie, 0))
    gmask = grank < TOPK_GROUPS
    masked = jnp.where(gmask[None], sel3, -jnp.inf).reshape(N_EXPERTS, tt)
    rio = lax.broadcasted_iota(I32, (N_EXPERTS, tt), 0)
    eid = (rio % N_GROUPS) * GROUP_SIZE + rio // N_GROUPS
    erank = jnp.zeros((N_EXPERTS, tt), I32)
    for rp in range(N_EXPERTS):
        ep = (rp % N_GROUPS) * GROUP_SIZE + rp // N_GROUPS
        row = masked[rp:rp + 1, :]
        tie = jnp.where(eid > ep, 1, 0)
        erank = erank + jnp.where(row > masked, 1, jnp.where(row == masked, tie, 0))
    smask = erank < TOP_K
    picked = jnp.where(smask, scores, 0.0)
    wsum = jnp.sum(picked, axis=0, keepdims=True)
    gd_ref[...] = picked / wsum * ROUTED_SCALE
    ones = jnp.where(smask, 1.0, 0.0)
    incl = jnp.dot(ones.astype(BF16), tri_ref[...], preferred_element_type=F32)
    pos = carry_scr[...] + incl - ones
    pos_ref[...] = jnp.where(smask, pos, -1.0).astype(I32)
    carry = carry_scr[...] + jnp.sum(ones, axis=1, keepdims=True)
    carry_scr[...] = carry
    cnt_ref[...] = jnp.broadcast_to(carry, (N_EXPERTS, LANES)).astype(I32)


def _router(logits_t, rb, tri, *, tt):
    n = logits_t.shape[1]
    tile = pl.BlockSpec((N_EXPERTS, tt), lambda i: (0, i))
    return pl.pallas_call(
        functools.partial(_router_kernel, tt=tt),
        grid=(n // tt,),
        in_specs=[tile, pl.BlockSpec((N_EXPERTS, 1), lambda i: (0, 0)), pl.BlockSpec((tt, tt), lambda i: (0, 0))],
        out_specs=[tile, tile, pl.BlockSpec((N_EXPERTS, LANES), lambda i: (0, 0))],
        out_shape=[
            jax.ShapeDtypeStruct((N_EXPERTS, n), I32),
            jax.ShapeDtypeStruct((N_EXPERTS, n), F32),
            jax.ShapeDtypeStruct((N_EXPERTS, LANES), I32),
        ],
        scratch_shapes=[pltpu.VMEM((N_EXPERTS, 1), F32)],
        compiler_params=_cparams(("arbitrary",)),
        name="router",
    )(logits_t, rb, tri)


def _slots_kernel(pos_ref, gd_ref, ps_ref, ltri_ref, dest_ref, gate_ref):
    pos = pos_ref[...]
    gd = gd_ref[...]
    chosen = pos >= 0
    ones = jnp.where(chosen, 1.0, 0.0).astype(BF16)
    before = jnp.dot(ltri_ref[...], ones, preferred_element_type=F32)
    dest_full = pos + ps_ref[...]
    for k in range(TOP_K):
        hit = jnp.where(chosen, before, -1.0) == float(k)
        dest_ref[k:k + 1, :] = jnp.sum(jnp.where(hit, dest_full, 0), axis=0, keepdims=True)
        gate_ref[k:k + 1, :] = jnp.sum(jnp.where(hit, gd, 0.0), axis=0, keepdims=True)


def _slots(pos, gd, pad_start, ltri, *, tt):
    n = pos.shape[1]
    tile = pl.BlockSpec((N_EXPERTS, tt), lambda i: (0, i))
    otile = pl.BlockSpec((TOP_K, tt), lambda i: (0, i))
    return pl.pallas_call(
        _slots_kernel,
        grid=(n // tt,),
        in_specs=[tile, tile, pl.BlockSpec((N_EXPERTS, 1), lambda i: (0, 0)),
                  pl.BlockSpec((N_EXPERTS, N_EXPERTS), lambda i: (0, 0))],
        out_specs=[otile, otile],
        out_shape=[jax.ShapeDtypeStruct((TOP_K, n), I32), jax.ShapeDtypeStruct((TOP_K, n), F32)],
        compiler_params=_cparams(("arbitrary",)),
        name="slots",
    )(pos, gd, pad_start, ltri)


def _dispatch_kernel(pad_end_ref, cnt_ref, dest_ref, rows_ref, xs_ref, zero_scr, zsem, sem, *, td, rpt, rb):
    i = pl.program_id(0)

    def zero_copy(r):
        return pltpu.make_async_copy(zero_scr, xs_ref.at[pl.ds(pad_end_ref[r] - rb, rb)], zsem)

    @pl.when(i == 0)
    def _():
        zero_scr[...] = jnp.zeros_like(zero_scr)

        def start(r, carry):
            @pl.when(cnt_ref[r] > 0)
            def _():
                zero_copy(r).start()
            return carry

        def wait(r, carry):
            @pl.when(cnt_ref[r] > 0)
            def _():
                zero_copy(r).wait()
            return carry

        lax.fori_loop(0, N_EXPERTS, start, 0)
        lax.fori_loop(0, N_EXPERTS, wait, 0)

    def token(t, carry):
        src = rows_ref.at[t]
        for k in range(TOP_K):
            pltpu.make_async_copy(src, xs_ref.at[dest_ref[t * TOP_K + k]], sem).start()
        return carry

    lax.fori_loop(0, td, token, 0)
    for k in range(TOP_K):
        pltpu.make_async_copy(rows_ref, xs_ref.at[pl.ds(0, td)], sem).wait()


def _dispatch(pad_end, counts, dest_flat, h1r, *, n_rows, td, rpt, rb):
    n = h1r.shape[0]
    grid_spec = pltpu.PrefetchScalarGridSpec(
        num_scalar_prefetch=2,
        grid=(n // td,),
        in_specs=[
            pl.BlockSpec((td * TOP_K,), lambda i, pe, cn: (i,), memory_space=pltpu.SMEM),
            pl.BlockSpec((td, rpt, LANES), lambda i, pe, cn: (i, 0, 0)),
        ],
        out_specs=pl.BlockSpec(memory_space=pl.ANY),
        scratch_shapes=[pltpu.VMEM((rb, rpt, LANES), F32), pltpu.SemaphoreType.DMA(()), pltpu.SemaphoreType.DMA(())],
    )
    return pl.pallas_call(
        functools.partial(_dispatch_kernel, td=td, rpt=rpt, rb=rb),
        grid_spec=grid_spec,
        out_shape=jax.ShapeDtypeStruct((n_rows, rpt, LANES), F32),
        compiler_params=_cparams(("arbitrary",)),
        name="dispatch",
    )(pad_end, counts, dest_flat, h1r)


def _expert_kernel(bidx_ref, bexp_ref, bnext_ref, nused_ref, xs_ref, wg_hbm, wu_hbm, wd_hbm, y_ref,
                   x_scr, wg_f32, wu_f32, wd_f32, wg_scr, wu_scr, wd_scr, wsem, *, rb, d):
    b = pl.program_id(0)

    def weight_copies(e):
        return (pltpu.make_async_copy(wg_hbm.at[e], wg_f32, wsem.at[0]),
                pltpu.make_async_copy(wu_hbm.at[e], wu_f32, wsem.at[1]),
                pltpu.make_async_copy(wd_hbm.at[e], wd_f32, wsem.at[2]))

    @pl.when(b == 0)
    def _():
        for cp in weight_copies(bexp_ref[0]):
            cp.start()

    new_expert = (b == 0) | (bexp_ref[b] != bexp_ref[jnp.maximum(b - 1, 0)])

    @pl.when(new_expert)
    def _():
        for cp in weight_copies(bexp_ref[b]):
            cp.wait()
        wg_scr[...] = wg_f32[...].astype(BF16)
        wu_scr[...] = wu_f32[...].astype(BF16)
        wd_scr[...] = wd_f32[...].astype(BF16)

        @pl.when(bnext_ref[b] >= 0)
        def _():
            for cp in weight_copies(bnext_ref[b]):
                cp.start()

    @pl.when(b < nused_ref[0])
    def _():
        xt = pltpu.einshape("rcl->crl", xs_ref[...])
        for ci in range(d // LANES):
            x_scr[:, ci * LANES:(ci + 1) * LANES] = xt[ci].astype(BF16)
        x = x_scr[...]
        g = jnp.dot(x, wg_scr[...], preferred_element_type=F32)
        u = jnp.dot(x, wu_scr[...], preferred_element_type=F32)
        hid = (g * jax.nn.sigmoid(g) * u).astype(BF16)
        y = jnp.dot(hid, wd_scr[...], preferred_element_type=F32)
        yt = jnp.stack([y[:, ci * LANES:(ci + 1) * LANES] for ci in range(d // LANES)], axis=0)
        y_ref[...] = pltpu.einshape("crl->rcl", yt)


def _experts(bidx, bexp, bnext, nused, xs, wg, wu, wd, *, rb, d):
    ypt = d // LANES
    n_blocks = xs.shape[0] // rb
    f = wg.shape[2]
    grid_spec = pltpu.PrefetchScalarGridSpec(
        num_scalar_prefetch=4,
        grid=(n_blocks,),
        in_specs=[
            pl.BlockSpec((rb, ypt, LANES), lambda b, bi, be, bn, nu: (bi[b], 0, 0)),
            pl.BlockSpec(memory_space=pl.ANY), pl.BlockSpec(memory_space=pl.ANY), pl.BlockSpec(memory_space=pl.ANY),
        ],
        out_specs=pl.BlockSpec((rb, ypt, LANES), lambda b, bi, be, bn, nu: (bi[b], 0, 0)),
        scratch_shapes=[pltpu.VMEM((rb, d), BF16),
                        pltpu.VMEM((d, f), F32), pltpu.VMEM((d, f), F32), pltpu.VMEM((f, d), F32),
                        pltpu.VMEM((d, f), BF16), pltpu.VMEM((d, f), BF16), pltpu.VMEM((f, d), BF16),
                        pltpu.SemaphoreType.DMA((3,))],
    )
    return pl.pallas_call(
        functools.partial(_expert_kernel, rb=rb, d=d),
        grid_spec=grid_spec,
        out_shape=jax.ShapeDtypeStruct((n_blocks * rb, ypt, LANES), F32),
        compiler_params=_cparams(("arbitrary",)),
        name="experts",
    )(bidx, bexp, bnext, nused, xs, wg, wu, wd)


def _combine_kernel(dest_ref, gate_ref, base_ref, g_ref, b_ref, y_ref, o_ref, buf, r_scr, sem, *, tc, ypt, nt):
    i = pl.program_id(0)
    slot = i % 2

    @pl.when(i < nt)
    def _():
        def token(t, carry):
            for k in range(TOP_K):
                dst = buf.at[slot * TOP_K + k, pl.ds(pl.multiple_of(t * ypt, ypt), ypt)]
                pltpu.make_async_copy(y_ref.at[dest_ref[t * TOP_K + k]], dst, sem.at[slot]).start()
            return carry

        lax.fori_loop(0, tc, token, 0)

    @pl.when(i > 0)
    def _():
        ps = 1 - slot
        rows_like = buf.at[slot * TOP_K]
        for k in range(TOP_K):
            pltpu.make_async_copy(rows_like, buf.at[ps * TOP_K + k], sem.at[ps]).wait()
        gate = gate_ref[...]
        gk = [jnp.broadcast_to(gate[:, k:k + 1], (tc, LANES)) for k in range(TOP_K)]
        for ci in range(ypt):
            lanes = slice(ci * LANES, (ci + 1) * LANES)
            acc = base_ref[:, lanes]
            for k in range(TOP_K):
                acc = acc + gk[k] * buf[ps * TOP_K + k, pl.ds(ci, tc, stride=ypt), :]
            r_scr[:, lanes] = acc
        o_ref[...] = _layer_norm(r_scr[...], g_ref[...], b_ref[...])


def _combine(dest_flat, gate_t, base, g, b, y, *, tc):
    n, d = base.shape
    ypt = d // LANES
    nt = n // tc
    prev = lambda i: (jnp.maximum(i - 1, 0), 0)
    return pl.pallas_call(
        functools.partial(_combine_kernel, tc=tc, ypt=ypt, nt=nt),
        grid=(nt + 1,),
        in_specs=[
            pl.BlockSpec((tc * TOP_K,), lambda i: (jnp.minimum(i, nt - 1),), memory_space=pltpu.SMEM),
            pl.BlockSpec((tc, TOP_K), prev),
            pl.BlockSpec((tc, d), prev),
            pl.BlockSpec((1, d), lambda i: (0, 0)),
            pl.BlockSpec((1, d), lambda i: (0, 0)),
            pl.BlockSpec(memory_space=pl.ANY),
        ],
        out_specs=pl.BlockSpec((tc, d), prev),
        out_shape=jax.ShapeDtypeStruct((n, d), F32),
        scratch_shapes=[
            pltpu.VMEM((2 * TOP_K, tc * ypt, LANES), F32),
            pltpu.VMEM((tc, d), F32),
            pltpu.SemaphoreType.DMA((2,)),
        ],
        compiler_params=_cparams(("arbitrary",)),
        name="combine",
    )(dest_flat, gate_t, base, g, b, y)


def _row_vec(v):
    return v.reshape(1, -1).astype(F32)


def _expert_rows(a):
    return a.reshape((N_GROUPS, GROUP_SIZE) + a.shape[1:]).swapaxes(0, 1).reshape(a.shape)


def _tile(n, want):
    t = min(n, want)
    assert n % t == 0
    return t


def kernel(x, ln_in_g, ln_in_b, w_in, conv_w, conv_b, conv_ln_g, conv_ln_b, rel_bias, w_out, ln1_g, ln1_b,
           w_router, router_bias, w_gate, w_up, w_down, w_shared_gate, w_shared_up, w_shared_down, ln2_g, ln2_b):
    bsz, s, d = x.shape
    n = bsz * s
    depth = w_in.shape[0]
    c = conv_w.shape[2]
    aw = rel_bias.shape[1] * HEAD_DIM
    assert c == aw and w_in.shape[2] == 2 * c + 3 * aw and w_router.shape[2] == N_EXPERTS
    alpha = (2 * depth) ** 0.25
    rpt = d // LANES
    rb = EXPERT_ROW_BLOCK
    n_blocks = -(-n * TOP_K // rb) + N_EXPERTS
    n_rows = n_blocks * rb
    tt = _tile(n, 1024)
    tri = jnp.triu(jnp.ones((tt, tt), BF16))
    ltri = jnp.tril(jnp.ones((N_EXPERTS, N_EXPERTS), BF16), -1)

    h = x
    for li in range(depth):
        hn, u, vt = _inproj(h, _row_vec(ln_in_g), _row_vec(ln_in_b), w_in[li, :, :2 * c + 2 * aw].astype(BF16),
                            w_in[li, :, 2 * c + 2 * aw:].T.astype(BF16), apply_ln=(li == 0), tm=_tile(s, 512), tn=c)
        conv_out = _conv(u, conv_w[li].astype(F32), _row_vec(conv_b[li]), _row_vec(conv_ln_g[li]),
                         _row_vec(conv_ln_b[li]), s=s, c=c, ts=_tile(s, 512))
        att_out = _attention(u, vt, _expand_rel_bias(rel_bias[li]), s=s, aw=aw, qb=PAD_ROWS // 2, qcol=2)
        wr_t = _expert_rows(w_router[li].T).astype(BF16)
        base, h1r, logits_t = _outproj(
            conv_out.reshape(n, c), att_out.reshape(n, aw), hn.reshape(n, d), w_out[li].astype(BF16),
            _row_vec(ln1_g[li]), _row_vec(ln1_b[li]), wr_t, w_shared_gate[li].astype(BF16),
            w_shared_up[li].astype(BF16), w_shared_down[li].astype(BF16), alpha=alpha, tm=_tile(n, 512))
        rbias = _expert_rows(router_bias[li].reshape(N_EXPERTS, 1)).astype(F32)
        pos, gd, cnt = _router(logits_t, rbias, tri, tt=tt)
        counts = cnt[:, 0]
        padded = (counts + rb - 1) // rb * rb
        pad_end = jnp.cumsum(padded).astype(I32)
        pad_start = pad_end - padded
        dest, gate = _slots(pos, gd, pad_start.reshape(N_EXPERTS, 1), ltri, tt=tt)
        dest_flat = dest.T.reshape(n * TOP_K)
        xs = _dispatch(pad_end, counts, dest_flat, h1r.reshape(n, rpt, LANES), n_rows=n_rows, td=_tile(n, 256),
                       rpt=rpt, rb=rb)
        n_used = pad_end[-1] // rb
        blk = jnp.minimum(jnp.arange(n_blocks, dtype=I32), n_used - 1)
        blk_row = jnp.sum((pad_end[None, :] <= (blk * rb)[:, None]).astype(I32), axis=1)
        row_ids = jnp.arange(N_EXPERTS, dtype=I32)
        expert_of_row = (row_ids % N_GROUPS) * GROUP_SIZE + row_ids // N_GROUPS
        later = lax.cummin(jnp.where(counts > 0, row_ids, N_EXPERTS)[::-1])[::-1]
        next_row = jnp.concatenate([later[1:], jnp.full((1,), N_EXPERTS, I32)])
        next_exp = jnp.where(next_row < N_EXPERTS, expert_of_row[jnp.minimum(next_row, N_EXPERTS - 1)], -1)
        y = _experts(blk, expert_of_row[blk_row], next_exp[blk_row], n_used.reshape(1), xs,
                     w_gate[li], w_up[li], w_down[li], rb=rb, d=d)
        h = _combine(dest_flat, gate.T, base, _row_vec(ln2_g[li]), _row_vec(ln2_b[li]), y,
                     tc=_tile(n, 128)).reshape(bsz, s, d)
    return h
```

```python
import functools

import jax
import jax.numpy as jnp
from jax import lax
from jax.experimental import pallas as pl
from jax.experimental.pallas import tpu as pltpu

CHUNK = 64
CONV_WIDTH = 31
HEAD_DIM = 64
LEFT_CHUNKS = 8
BAND = (LEFT_CHUNKS + 1) * CHUNK
MAX_REL = 256
N_EXPERTS = 64
TOP_K = 8
N_GROUPS = 8
GROUP_SIZE = N_EXPERTS // N_GROUPS
TOPK_GROUPS = 4
ROUTED_SCALE = 2.5
LN_EPS = 1e-5

LANES = 128
SUBLANES = 8
VMEM_LIMIT = 56 * 1024 * 1024

PAD_ROWS = LEFT_CHUNKS * CHUNK
HALO_ROWS = 32
CONV_ROW_CHUNK = 64
EXPERT_ROW_BLOCK = 256
NEG_BIG = -1e30

F32 = jnp.float32
BF16 = jnp.bfloat16
I32 = jnp.int32


def _cparams(semantics, vmem=VMEM_LIMIT):
    return pltpu.CompilerParams(dimension_semantics=semantics, vmem_limit_bytes=vmem)


def _layer_norm(x, g, b):
    mu = jnp.mean(x, axis=-1, keepdims=True)
    xc = x - mu
    var = jnp.mean(xc * xc, axis=-1, keepdims=True)
    return xc * lax.rsqrt(var + LN_EPS) * g + b


def _inproj_kernel(x_ref, g_ref, b_ref, w_ref, wvt_ref, h_ref, u_ref, vt_ref, hn_ref, *, apply_ln, n_u, tn):
    i = pl.program_id(1)
    j = pl.program_id(2)

    @pl.when(i == 0)
    def _():
        u_ref[...] = jnp.zeros_like(u_ref)
        vt_ref[...] = jnp.zeros_like(vt_ref)

    @pl.when((i > 0) & (j == 0))
    def _():
        x = x_ref[0]
        h = _layer_norm(x, g_ref[...], b_ref[...]) if apply_ln else x
        h_ref[0] = h
        hn_ref[...] = h.astype(BF16)

    @pl.when((i > 0) & (j < n_u))
    def _():
        cols = pl.ds(pl.multiple_of(j * tn, tn), tn)
        u_ref[0] = jnp.dot(hn_ref[...], w_ref[:, cols], preferred_element_type=F32).astype(BF16)

    @pl.when((i > 0) & (j == n_u))
    def _():
        vt = lax.dot_general(wvt_ref[...], hn_ref[...], (((1,), (1,)), ((), ())), preferred_element_type=F32)
        vt_ref[0] = vt.astype(BF16)


def _inproj(x, g, b, w_u, w_vt, *, apply_ln, tm, tn):
    bsz, s, d = x.shape
    width = w_u.shape[1]
    aw = w_vt.shape[0]
    assert PAD_ROWS % tm == 0 and s % tm == 0 and width % tn == 0
    pad_blocks = PAD_ROWS // tm
    n_u = width // tn
    grid = (bsz, s // tm + pad_blocks, n_u + 1)
    row = lambda bi, i, j: (bi, jnp.maximum(i - pad_blocks, 0), 0)
    const = lambda shape: pl.BlockSpec(shape, lambda bi, i, j: (0,) * len(shape), pipeline_mode=pl.Buffered(1))
    return pl.pallas_call(
        functools.partial(_inproj_kernel, apply_ln=apply_ln, n_u=n_u, tn=tn),
        grid=grid,
        in_specs=[
            pl.BlockSpec((1, tm, d), row),
            const((1, d)), const((1, d)),
            const((d, width)),
            const((aw, d)),
        ],
        out_specs=[
            pl.BlockSpec((1, tm, d), row),
            pl.BlockSpec((1, tm, tn), lambda bi, i, j: (bi, i, jnp.minimum(j, n_u - 1))),
            pl.BlockSpec((1, aw, tm), lambda bi, i, j: (bi, 0, i)),
        ],
        out_shape=[
            jax.ShapeDtypeStruct((bsz, s, d), F32),
            jax.ShapeDtypeStruct((bsz, s + PAD_ROWS, width), BF16),
            jax.ShapeDtypeStruct((bsz, aw, s + PAD_ROWS), BF16),
        ],
        scratch_shapes=[pltpu.VMEM((tm, d), BF16)],
        compiler_params=_cparams(("arbitrary", "arbitrary", "arbitrary")),
        name="inproj",
    )(x, g, b, w_u, w_vt)


def _conv_kernel(ah_ref, gh_ref, at_ref, gt_ref, w_ref, cb_ref, lg_ref, lb_ref, o_ref, v_scr, y_scr, *, ts, c):
    v_scr[0:HALO_ROWS, :] = ah_ref[0].astype(F32) * jax.nn.sigmoid(gh_ref[0].astype(F32))
    v_scr[HALO_ROWS:HALO_ROWS + ts, :] = at_ref[0].astype(F32) * jax.nn.sigmoid(gt_ref[0].astype(F32))
    first_tap = HALO_ROWS - (CONV_WIDTH - 1)
    win_rows = CONV_ROW_CHUNK + HALO_ROWS

    def row_chunk(rc, carry):
        r0 = pl.multiple_of(rc * CONV_ROW_CHUNK, CONV_ROW_CHUNK)
        for ci in range(c // LANES):
            lanes = slice(ci * LANES, (ci + 1) * LANES)
            win = v_scr[pl.ds(r0, win_rows), lanes]
            shifted = [win] + [pltpu.roll(win, win_rows - sh, 0) for sh in range(1, SUBLANES)]
            wc = w_ref[:, lanes]
            acc = jnp.zeros((CONV_ROW_CHUNK, LANES), F32)
            for k in range(CONV_WIDTH):
                start = (first_tap + k) // SUBLANES * SUBLANES
                acc = acc + wc[k:k + 1, :] * shifted[(first_tap + k) % SUBLANES][start:start + CONV_ROW_CHUNK, :]
            y_scr[pl.ds(r0, CONV_ROW_CHUNK), lanes] = acc + cb_ref[:, lanes]
        return carry

    lax.fori_loop(0, ts // CONV_ROW_CHUNK, row_chunk, 0)
    y = _layer_norm(y_scr[...], lg_ref[...], lb_ref[...])
    o_ref[0] = (y * jax.nn.sigmoid(y)).astype(BF16)


def _conv(u, conv_w, conv_b, ln_g, ln_b, *, s, c, ts):
    bsz = u.shape[0]
    assert PAD_ROWS % ts == 0 and s % ts == 0 and ts % HALO_ROWS == 0 and ts % CONV_ROW_CHUNK == 0
    hb = ts // HALO_ROWS
    pb = PAD_ROWS // ts
    halo = lambda col: pl.BlockSpec((1, HALO_ROWS, c), lambda bi, i: (bi, (pb + i) * hb - 1, col))
    tile = lambda col: pl.BlockSpec((1, ts, c), lambda bi, i: (bi, pb + i, col))
    vec = lambda rows: pl.BlockSpec((rows, c), lambda bi, i: (0, 0))
    return pl.pallas_call(
        functools.partial(_conv_kernel, ts=ts, c=c),
        grid=(bsz, s // ts),
        in_specs=[halo(0), halo(1), tile(0), tile(1), vec(CONV_WIDTH), vec(1), vec(1), vec(1)],
        out_specs=pl.BlockSpec((1, ts, c), lambda bi, i: (bi, i, 0)),
        out_shape=jax.ShapeDtypeStruct((bsz, s, c), BF16),
        scratch_shapes=[pltpu.VMEM((ts + HALO_ROWS, c), F32), pltpu.VMEM((ts, c), F32)],
        compiler_params=_cparams(("arbitrary", "arbitrary")),
        name="conv",
    )(u, u, u, u, conv_w, conv_b, ln_g, ln_b)


def _attn_kernel(q_ref, k0_ref, k1_ref, k2_ref, v0_ref, v1_ref, v2_ref, bias_ref, o_ref, kw_scr, vw_scr, st_scr, p_scr,
                 *, qb, aw):
    i = pl.program_id(1)
    for j, (kr, vr) in enumerate(((k0_ref, v0_ref), (k1_ref, v1_ref), (k2_ref, v2_ref))):
        kw_scr[j * qb:(j + 1) * qb, :] = kr[0]
        vw_scr[:, j * qb:(j + 1) * qb] = vr[0]
    lane = lax.broadcasted_iota(I32, (1, LANES), 1)
    first_head = lane < HEAD_DIM
    key_iota = lax.broadcasted_iota(I32, (BAND, LANES), 0)
    scale = jnp.asarray(HEAD_DIM ** -0.5, BF16)
    zero = jnp.zeros((), BF16)

    @pl.when((pl.program_id(0) == 0) & (i == 0))
    def _():
        p_scr[...] = jnp.zeros_like(p_scr)

    def run(masked):
        def head_pair(hp, carry):
            lo = pl.multiple_of(hp * LANES, LANES)
            qp = q_ref[0, :, pl.ds(lo, LANES)] * scale
            vtp = vw_scr[pl.ds(lo, LANES), :]
            bias = bias_ref[hp]
            n_chunks = qb // CHUNK
            for ca in range(n_chunks):
                qc = qp[ca * CHUNK:(ca + 1) * CHUNK]
                qs = jnp.concatenate([jnp.where(first_head, qc, zero), jnp.where(first_head, zero, qc)], axis=0)
                st = lax.dot_general(kw_scr[ca * CHUNK:ca * CHUNK + BAND, pl.ds(lo, LANES)], qs,
                                     (((1,), (1,)), ((), ())), preferred_element_type=F32) + bias
                if masked:
                    st = jnp.where(key_iota >= PAD_ROWS - (i * qb + ca * CHUNK), st, NEG_BIG)
                st_scr[ca] = st
            inv_l = []
            for ca in range(n_chunks):
                st = st_scr[ca]
                m = jnp.max(st, axis=0, keepdims=True)
                e = jnp.exp(st - m)
                inv_l.append(1.0 / jnp.sum(e, axis=0, keepdims=True))
                p_scr[ca, ca * CHUNK:ca * CHUNK + BAND, :] = e.astype(BF16)
            for ca in range(n_chunks):
                ot = jnp.dot(vtp, p_scr[ca], preferred_element_type=F32) * inv_l[ca]
                t = ot.T
                o_ref[0, ca * CHUNK:(ca + 1) * CHUNK, pl.ds(lo, LANES)] = jnp.where(
                    first_head, t[0:CHUNK], t[CHUNK:]).astype(BF16)
            return carry

        lax.fori_loop(0, aw // LANES, head_pair, 0)

    first_valid_block = PAD_ROWS // qb

    @pl.when(i < first_valid_block)
    def _():
        run(True)

    @pl.when(i >= first_valid_block)
    def _():
        run(False)


def _attention(u, vt, bias_t, *, s, aw, qb, qcol):
    bsz = u.shape[0]
    assert PAD_ROWS == 2 * qb and s % qb == 0 and aw % LANES == 0
    pb = PAD_ROWS // qb
    kspec = lambda j: pl.BlockSpec((1, qb, aw), lambda bi, i: (bi, i + j, qcol + 1))
    vspec = lambda j: pl.BlockSpec((1, aw, qb), lambda bi, i: (bi, 0, i + j))
    return pl.pallas_call(
        functools.partial(_attn_kernel, qb=qb, aw=aw),
        grid=(bsz, s // qb),
        in_specs=[pl.BlockSpec((1, qb, aw), lambda bi, i: (bi, pb + i, qcol))]
        + [kspec(j) for j in range(3)] + [vspec(j) for j in range(3)]
        + [pl.BlockSpec(bias_t.shape, lambda bi, i: (0, 0, 0))],
        out_specs=pl.BlockSpec((1, qb, aw), lambda bi, i: (bi, i, 0)),
        out_shape=jax.ShapeDtypeStruct((bsz, s, aw), BF16),
        scratch_shapes=[pltpu.VMEM((3 * qb, aw), BF16), pltpu.VMEM((aw, 3 * qb), BF16),
                        pltpu.VMEM((qb // CHUNK, BAND, LANES), F32), pltpu.VMEM((qb // CHUNK, 3 * qb, LANES), BF16)],
        compiler_params=_cparams(("arbitrary", "arbitrary")),
        name="attn",
    )(u, u, u, u, vt, vt, vt, bias_t)


def _expand_rel_bias(rel_bias):
    tail = jnp.broadcast_to(rel_bias[:, -1:], (rel_bias.shape[0], PAD_ROWS + CHUNK - 1 - MAX_REL))
    rev = jnp.concatenate([rel_bias, tail], axis=1)[:, ::-1].astype(F32)
    bias = jnp.stack([rev[:, CHUNK - 1 - i:CHUNK - 1 - i + BAND] for i in range(CHUNK)], axis=1)
    heads = bias.shape[0]
    return bias.reshape(heads // 2, 2, CHUNK, BAND).transpose(0, 3, 1, 2).reshape(heads // 2, BAND, 2 * CHUNK)


def _rows_to_tiles(h, ref, *, rows, d):
    for ci in range(d // LANES):
        ref[pl.ds(ci, rows, stride=d // LANES), :] = h[:, ci * LANES:(ci + 1) * LANES]


def _outproj_kernel(conv_ref, att_ref, h_ref, wo_ref, g_ref, b_ref, wr_ref, h1r_ref, lg_ref, *, alpha, c, tm, d):
    mixed = jnp.dot(conv_ref[...], wo_ref[0:c, :], preferred_element_type=F32)
    mixed = mixed + jnp.dot(att_ref[...], wo_ref[c:, :], preferred_element_type=F32)
    h1 = _layer_norm(alpha * h_ref[...] + mixed, g_ref[...], b_ref[...])
    h1b = h1.astype(BF16)
    lg_ref[...] = lax.dot_general(wr_ref[...], h1b, (((1,), (1,)), ((), ())), preferred_element_type=F32)
    _rows_to_tiles(h1, h1r_ref, rows=tm, d=d)


def _outproj(conv_out, att_out, h, wo, g, b, wr_t, *, alpha, tm):
    n, d = h.shape
    c = conv_out.shape[1]
    rpt = d // LANES
    const = lambda shape: pl.BlockSpec(shape, lambda i: (0,) * len(shape), pipeline_mode=pl.Buffered(1))
    return pl.pallas_call(
        functools.partial(_outproj_kernel, alpha=alpha, c=c, tm=tm, d=d),
        grid=(n // tm,),
        in_specs=[
            pl.BlockSpec((tm, c), lambda i: (i, 0)),
            pl.BlockSpec((tm, att_out.shape[1]), lambda i: (i, 0)),
            pl.BlockSpec((tm, d), lambda i: (i, 0)),
            const(wo.shape), const((1, d)), const((1, d)), const(wr_t.shape),
        ],
        out_specs=[
            pl.BlockSpec((tm * rpt, LANES), lambda i: (i, 0)),
            pl.BlockSpec((N_EXPERTS, tm), lambda i: (0, i)),
        ],
        out_shape=[
            jax.ShapeDtypeStruct((n * rpt, LANES), F32),
            jax.ShapeDtypeStruct((N_EXPERTS, n), F32),
        ],
        compiler_params=_cparams(("arbitrary",)),
        name="outproj",
    )(conv_out, att_out, h, wo, g, b, wr_t)


def _router_kernel(lg_ref, rb_ref, tri_ref, pos_ref, gd_ref, cnt_ref, carry_scr, *, tt):
    i = pl.program_id(0)

    @pl.when(i == 0)
    def _():
        carry_scr[...] = jnp.zeros_like(carry_scr)

    scores = jax.nn.sigmoid(lg_ref[...])
    sel = scores + rb_ref[...]
    sel3 = sel.reshape(GROUP_SIZE, N_GROUPS, tt)
    jio = lax.broadcasted_iota(I32, (GROUP_SIZE, N_GROUPS, tt), 0)
    m1 = jnp.max(sel3, axis=0)
    first = jnp.min(jnp.where(sel3 == m1[None], jio, GROUP_SIZE), axis=0)
    m2 = jnp.max(jnp.where(jio == first[None], -jnp.inf, sel3), axis=0)
    gs = m1 + m2
    gio = lax.broadcasted_iota(I32, (N_GROUPS, tt), 0)
    grank = jnp.zeros((N_GROUPS, tt), I32)
    for gp in range(N_GROUPS):
        row = gs[gp:gp + 1, :]
        tie = jnp.where(gio > gp, 1, 0)
        grank = grank + jnp.where(row > gs, 1, jnp.where(row == gs, tie, 0))
    gmask = grank < TOPK_GROUPS
    masked = jnp.where(gmask[None], sel3, -jnp.inf).reshape(N_EXPERTS, tt)
    rio = lax.broadcasted_iota(I32, (N_EXPERTS, tt), 0)
    eid = (rio % N_GROUPS) * GROUP_SIZE + rio // N_GROUPS
    erank = jnp.zeros((N_EXPERTS, tt), I32)
    for rp in range(N_EXPERTS):
        ep = (rp % N_GROUPS) * GROUP_SIZE + rp // N_GROUPS
        row = masked[rp:rp + 1, :]
        tie = jnp.where(eid > ep, 1, 0)
        erank = erank + jnp.where(row > masked, 1, jnp.where(row == masked, tie, 0))
    smask = erank < TOP_K
    picked = jnp.where(smask, scores, 0.0)
    wsum = jnp.sum(picked, axis=0, keepdims=True)
    gd_ref[...] = picked / wsum * ROUTED_SCALE
    ones = jnp.where(smask, 1.0, 0.0)
    incl = jnp.dot(ones.astype(BF16), tri_ref[...], preferred_element_type=F32)
    pos = carry_scr[...] + incl - ones
    pos_ref[...] = jnp.where(smask, pos, -1.0).astype(I32)
    carry = carry_scr[...] + jnp.sum(ones, axis=1, keepdims=True)
    carry_scr[...] = carry
    cnt_ref[...] = jnp.broadcast_to(carry, (N_EXPERTS, LANES)).astype(I32)


def _router(logits_t, rb, tri, *, tt):
    n = logits_t.shape[1]
    tile = pl.BlockSpec((N_EXPERTS, tt), lambda i: (0, i))
    return pl.pallas_call(
        functools.partial(_router_kernel, tt=tt),
        grid=(n // tt,),
        in_specs=[tile, pl.BlockSpec((N_EXPERTS, 1), lambda i: (0, 0)), pl.BlockSpec((tt, tt), lambda i: (0, 0))],
        out_specs=[tile, tile, pl.BlockSpec((N_EXPERTS, LANES), lambda i: (0, 0))],
        out_shape=[
            jax.ShapeDtypeStruct((N_EXPERTS, n), I32),
            jax.ShapeDtypeStruct((N_EXPERTS, n), F32),
            jax.ShapeDtypeStruct((N_EXPERTS, LANES), I32),
        ],
        scratch_shapes=[pltpu.VMEM((N_EXPERTS, 1), F32)],
        compiler_params=_cparams(("arbitrary",)),
        name="router",
    )(logits_t, rb, tri)


def _slots_kernel(pos_ref, gd_ref, ps_ref, ltri_ref, dest_ref, gate_ref):
    pos = pos_ref[...]
    gd = gd_ref[...]
    chosen = pos >= 0
    ones = jnp.where(chosen, 1.0, 0.0).astype(BF16)
    before = jnp.dot(ltri_ref[...], ones, preferred_element_type=F32)
    dest_full = pos + ps_ref[...]
    for k in range(TOP_K):
        hit = jnp.where(chosen, before, -1.0) == float(k)
        dest_ref[k:k + 1, :] = jnp.sum(jnp.where(hit, dest_full, 0), axis=0, keepdims=True)
        gate_ref[k:k + 1, :] = jnp.sum(jnp.where(hit, gd, 0.0), axis=0, keepdims=True)


def _slots(pos, gd, pad_start, ltri, *, tt):
    n = pos.shape[1]
    tile = pl.BlockSpec((N_EXPERTS, tt), lambda i: (0, i))
    otile = pl.BlockSpec((TOP_K, tt), lambda i: (0, i))
    return pl.pallas_call(
        _slots_kernel,
        grid=(n // tt,),
        in_specs=[tile, tile, pl.BlockSpec((N_EXPERTS, 1), lambda i: (0, 0)),
                  pl.BlockSpec((N_EXPERTS, N_EXPERTS), lambda i: (0, 0))],
        out_specs=[otile, otile],
        out_shape=[jax.ShapeDtypeStruct((TOP_K, n), I32), jax.ShapeDtypeStruct((TOP_K, n), F32)],
        compiler_params=_cparams(("arbitrary",)),
        name="slots",
    )(pos, gd, pad_start, ltri)


def _dispatch_kernel(pad_end_ref, cnt_ref, dest_ref, rows_ref, wsg_ref, wsu_ref, wsd_ref, xs_ref, base_ref,
                     zero_scr, h_scr, zsem, sem, *, td, rpt, rb, alpha):
    i = pl.program_id(0)

    def zero_copy(r):
        return pltpu.make_async_copy(zero_scr, xs_ref.at[pl.ds(pad_end_ref[r] - rb, rb)], zsem)

    @pl.when(i == 0)
    def _():
        zero_scr[...] = jnp.zeros_like(zero_scr)

        def start(r, carry):
            @pl.when(cnt_ref[r] > 0)
            def _():
                zero_copy(r).start()
            return carry

        def wait(r, carry):
            @pl.when(cnt_ref[r] > 0)
            def _():
                zero_copy(r).wait()
            return carry

        lax.fori_loop(0, N_EXPERTS, start, 0)
        lax.fori_loop(0, N_EXPERTS, wait, 0)

    def token(t, carry):
        src = rows_ref.at[t]
        for k in range(TOP_K):
            pltpu.make_async_copy(src, xs_ref.at[dest_ref[t * TOP_K + k]], sem).start()
        return carry

    lax.fori_loop(0, td, token, 0)

    ht = jnp.swapaxes(rows_ref[...], 0, 1)
    for ci in range(rpt):
        h_scr[:, ci * LANES:(ci + 1) * LANES] = ht[ci]
    h1 = h_scr[...]
    h1b = h1.astype(BF16)
    sg = jnp.dot(h1b, wsg_ref[...], preferred_element_type=F32)
    su = jnp.dot(h1b, wsu_ref[...], preferred_element_type=F32)
    hid = (sg * jax.nn.sigmoid(sg) * su).astype(BF16)
    base_ref[...] = alpha * h1 + jnp.dot(hid, wsd_ref[...], preferred_element_type=F32)

    for k in range(TOP_K):
        pltpu.make_async_copy(rows_ref, xs_ref.at[pl.ds(0, td)], sem).wait()


def _dispatch(pad_end, counts, dest_flat, h1r, wsg, wsu, wsd, *, n_rows, td, rpt, rb, alpha):
    n = h1r.shape[0]
    d = rpt * LANES
    const = lambda shape: pl.BlockSpec(shape, lambda i, pe, cn: (0,) * len(shape), pipeline_mode=pl.Buffered(1))
    grid_spec = pltpu.PrefetchScalarGridSpec(
        num_scalar_prefetch=2,
        grid=(n // td,),
        in_specs=[
            pl.BlockSpec((td * TOP_K,), lambda i, pe, cn: (i,), memory_space=pltpu.SMEM),
            pl.BlockSpec((td, rpt, LANES), lambda i, pe, cn: (i, 0, 0)),
            const(wsg.shape), const(wsu.shape), const(wsd.shape),
        ],
        out_specs=[pl.BlockSpec(memory_space=pl.ANY), pl.BlockSpec((td, d), lambda i, pe, cn: (i, 0))],
        scratch_shapes=[pltpu.VMEM((rb, rpt, LANES), F32), pltpu.VMEM((td, d), F32),
                        pltpu.SemaphoreType.DMA(()), pltpu.SemaphoreType.DMA(())],
    )
    return pl.pallas_call(
        functools.partial(_dispatch_kernel, td=td, rpt=rpt, rb=rb, alpha=alpha),
        grid_spec=grid_spec,
        out_shape=[jax.ShapeDtypeStruct((n_rows, rpt, LANES), F32), jax.ShapeDtypeStruct((n, d), F32)],
        compiler_params=_cparams(("arbitrary",)),
        name="dispatch",
    )(pad_end, counts, dest_flat, h1r, wsg, wsu, wsd)


def _expert_kernel(bidx_ref, bexp_ref, bnext_ref, nused_ref, xs_ref, wg_hbm, wu_hbm, wd_hbm, y_ref,
                   x_scr, wg_f32, wu_f32, wd_f32, wg_scr, wu_scr, wd_scr, wsem, *, rb, d):
    b = pl.program_id(0)

    def weight_copies(e):
        return (pltpu.make_async_copy(wg_hbm.at[e], wg_f32, wsem.at[0]),
                pltpu.make_async_copy(wu_hbm.at[e], wu_f32, wsem.at[1]),
                pltpu.make_async_copy(wd_hbm.at[e], wd_f32, wsem.at[2]))

    @pl.when(b == 0)
    def _():
        for cp in weight_copies(bexp_ref[0]):
            cp.start()

    new_expert = (b == 0) | (bexp_ref[b] != bexp_ref[jnp.maximum(b - 1, 0)])

    @pl.when(new_expert)
    def _():
        for cp in weight_copies(bexp_ref[b]):
            cp.wait()
        wg_scr[...] = wg_f32[...].astype(BF16)
        wu_scr[...] = wu_f32[...].astype(BF16)
        wd_scr[...] = wd_f32[...].astype(BF16)

        @pl.when(bnext_ref[b] >= 0)
        def _():
            for cp in weight_copies(bnext_ref[b]):
                cp.start()

    @pl.when(b < nused_ref[0])
    def _():
        xt = jnp.swapaxes(xs_ref[...], 0, 1)
        for ci in range(d // LANES):
            x_scr[:, ci * LANES:(ci + 1) * LANES] = xt[ci].astype(BF16)
        x = x_scr[...]
        g = jnp.dot(x, wg_scr[...], preferred_element_type=F32)
        u = jnp.dot(x, wu_scr[...], preferred_element_type=F32)
        hid = (g * jax.nn.sigmoid(g) * u).astype(BF16)
        y = jnp.dot(hid, wd_scr[...], preferred_element_type=F32)
        yt = jnp.stack([y[:, ci * LANES:(ci + 1) * LANES] for ci in range(d // LANES)], axis=0)
        y_ref[...] = jnp.swapaxes(yt, 0, 1)


def _experts(bidx, bexp, bnext, nused, xs, wg, wu, wd, *, rb, d):
    ypt = d // LANES
    n_blocks = xs.shape[0] // rb
    f = wg.shape[2]
    grid_spec = pltpu.PrefetchScalarGridSpec(
        num_scalar_prefetch=4,
        grid=(n_blocks,),
        in_specs=[
            pl.BlockSpec((rb, ypt, LANES), lambda b, bi, be, bn, nu: (bi[b], 0, 0)),
            pl.BlockSpec(memory_space=pl.ANY), pl.BlockSpec(memory_space=pl.ANY), pl.BlockSpec(memory_space=pl.ANY),
        ],
        out_specs=pl.BlockSpec((rb, ypt, LANES), lambda b, bi, be, bn, nu: (bi[b], 0, 0)),
        scratch_shapes=[pltpu.VMEM((rb, d), BF16),
                        pltpu.VMEM((d, f), F32), pltpu.VMEM((d, f), F32), pltpu.VMEM((f, d), F32),
                        pltpu.VMEM((d, f), BF16), pltpu.VMEM((d, f), BF16), pltpu.VMEM((f, d), BF16),
                        pltpu.SemaphoreType.DMA((3,))],
    )
    return pl.pallas_call(
        functools.partial(_expert_kernel, rb=rb, d=d),
        grid_spec=grid_spec,
        out_shape=jax.ShapeDtypeStruct((n_blocks * rb, ypt, LANES), F32),
        compiler_params=_cparams(("arbitrary",)),
        name="experts",
    )(bidx, bexp, bnext, nused, xs, wg, wu, wd)


def _combine_kernel(dest_ref, gate_ref, base_ref, g_ref, b_ref, y_ref, o_ref, buf, r_scr, sem, *, tc, ypt, nt):
    i = pl.program_id(0)
    slot = i % 2

    @pl.when(i < nt)
    def _():
        def token(t, carry):
            for k in range(TOP_K):
                dst = buf.at[slot * TOP_K + k, pl.ds(pl.multiple_of(t * ypt, ypt), ypt)]
                pltpu.make_async_copy(y_ref.at[dest_ref[t * TOP_K + k]], dst, sem.at[slot]).start()
            return carry

        lax.fori_loop(0, tc, token, 0)

    @pl.when(i > 0)
    def _():
        ps = 1 - slot
        rows_like = buf.at[slot * TOP_K]
        for k in range(TOP_K):
            pltpu.make_async_copy(rows_like, buf.at[ps * TOP_K + k], sem.at[ps]).wait()
        gate = gate_ref[...]
        gk = [jnp.broadcast_to(gate[:, k:k + 1], (tc, LANES)) for k in range(TOP_K)]
        for ci in range(ypt):
            lanes = slice(ci * LANES, (ci + 1) * LANES)
            acc = base_ref[:, lanes]
            for k in range(TOP_K):
                acc = acc + gk[k] * buf[ps * TOP_K + k, pl.ds(ci, tc, stride=ypt), :]
            r_scr[:, lanes] = acc
        o_ref[...] = _layer_norm(r_scr[...], g_ref[...], b_ref[...])


def _combine(dest_flat, gate_t, base, g, b, y, *, tc):
    n, d = base.shape
    ypt = d // LANES
    nt = n // tc
    prev = lambda i: (jnp.maximum(i - 1, 0), 0)
    return pl.pallas_call(
        functools.partial(_combine_kernel, tc=tc, ypt=ypt, nt=nt),
        grid=(nt + 1,),
        in_specs=[
            pl.BlockSpec((tc * TOP_K,), lambda i: (jnp.minimum(i, nt - 1),), memory_space=pltpu.SMEM),
            pl.BlockSpec((tc, TOP_K), prev),
            pl.BlockSpec((tc, d), prev),
            pl.BlockSpec((1, d), lambda i: (0, 0)),
            pl.BlockSpec((1, d), lambda i: (0, 0)),
            pl.BlockSpec(memory_space=pl.ANY),
        ],
        out_specs=pl.BlockSpec((tc, d), prev),
        out_shape=jax.ShapeDtypeStruct((n, d), F32),
        scratch_shapes=[
            pltpu.VMEM((2 * TOP_K, tc * ypt, LANES), F32),
            pltpu.VMEM((tc, d), F32),
            pltpu.SemaphoreType.DMA((2,)),
        ],
        compiler_params=_cparams(("arbitrary",)),
        name="combine",
    )(dest_flat, gate_t, base, g, b, y)


def _row_vec(v):
    return v.reshape(1, -1).astype(F32)


def _expert_rows(a):
    return a.reshape((N_GROUPS, GROUP_SIZE) + a.shape[1:]).swapaxes(0, 1).reshape(a.shape)


def _tile(n, want):
    t = min(n, want)
    assert n % t == 0
    return t


def kernel(x, ln_in_g, ln_in_b, w_in, conv_w, conv_b, conv_ln_g, conv_ln_b, rel_bias, w_out, ln1_g, ln1_b,
           w_router, router_bias, w_gate, w_up, w_down, w_shared_gate, w_shared_up, w_shared_down, ln2_g, ln2_b):
    bsz, s, d = x.shape
    n = bsz * s
    depth = w_in.shape[0]
    c = conv_w.shape[2]
    aw = rel_bias.shape[1] * HEAD_DIM
    assert c == aw and w_in.shape[2] == 2 * c + 3 * aw and w_router.shape[2] == N_EXPERTS
    alpha = (2 * depth) ** 0.25
    rpt = d // LANES
    rb = EXPERT_ROW_BLOCK
    n_blocks = -(-n * TOP_K // rb) + N_EXPERTS
    n_rows = n_blocks * rb
    tt = _tile(n, 1024)
    tri = jnp.triu(jnp.ones((tt, tt), BF16))
    ltri = jnp.tril(jnp.ones((N_EXPERTS, N_EXPERTS), BF16), -1)

    h = x
    for li in range(depth):
        hn, u, vt = _inproj(h, _row_vec(ln_in_g), _row_vec(ln_in_b), w_in[li, :, :2 * c + 2 * aw].astype(BF16),
                            w_in[li, :, 2 * c + 2 * aw:].T.astype(BF16), apply_ln=(li == 0), tm=_tile(s, 512), tn=c)
        conv_out = _conv(u, conv_w[li].astype(F32), _row_vec(conv_b[li]), _row_vec(conv_ln_g[li]),
                         _row_vec(conv_ln_b[li]), s=s, c=c, ts=_tile(s, 512))
        att_out = _attention(u, vt, _expand_rel_bias(rel_bias[li]), s=s, aw=aw, qb=PAD_ROWS // 2, qcol=2)
        wr_t = _expert_rows(w_router[li].T).astype(BF16)
        h1r, logits_t = _outproj(
            conv_out.reshape(n, c), att_out.reshape(n, aw), hn.reshape(n, d), w_out[li].astype(BF16),
            _row_vec(ln1_g[li]), _row_vec(ln1_b[li]), wr_t, alpha=alpha, tm=_tile(n, 512))
        rbias = _expert_rows(router_bias[li].reshape(N_EXPERTS, 1)).astype(F32)
        pos, gd, cnt = _router(logits_t, rbias, tri, tt=tt)
        counts = cnt[:, 0]
        padded = (counts + rb - 1) // rb * rb
        pad_end = jnp.cumsum(padded).astype(I32)
        pad_start = pad_end - padded
        dest, gate = _slots(pos, gd, pad_start.reshape(N_EXPERTS, 1), ltri, tt=tt)
        dest_flat = dest.T.reshape(n * TOP_K)
        xs, base = _dispatch(pad_end, counts, dest_flat, h1r.reshape(n, rpt, LANES),
                             w_shared_gate[li].astype(BF16), w_shared_up[li].astype(BF16),
                             w_shared_down[li].astype(BF16), n_rows=n_rows, td=_tile(n, 256), rpt=rpt, rb=rb,
                             alpha=alpha)
        n_used = pad_end[-1] // rb
        blk = jnp.minimum(jnp.arange(n_blocks, dtype=I32), n_used - 1)
        blk_row = jnp.sum((pad_end[None, :] <= (blk * rb)[:, None]).astype(I32), axis=1)
        row_ids = jnp.arange(N_EXPERTS, dtype=I32)
        expert_of_row = (row_ids % N_GROUPS) * GROUP_SIZE + row_ids // N_GROUPS
        candidate = (row_ids[None, :] > row_ids[:, None]) & (counts[None, :] > 0)
        next_row = jnp.min(jnp.where(candidate, row_ids[None, :], N_EXPERTS), axis=1)
        next_exp = jnp.where(next_row < N_EXPERTS, expert_of_row[jnp.minimum(next_row, N_EXPERTS - 1)], -1)
        y = _experts(blk, expert_of_row[blk_row], next_exp[blk_row], n_used.reshape(1), xs,
                     w_gate[li], w_up[li], w_down[li], rb=rb, d=d)
        h = _combine(dest_flat, gate.T, base, _row_vec(ln2_g[li]), _row_vec(ln2_b[li]), y,
                     tc=_tile(n, 128)).reshape(bsz, s, d)
    return h
```

```python
import functools

import jax
import jax.numpy as jnp
from jax import lax
from jax.experimental import pallas as pl
from jax.experimental.pallas import tpu as pltpu

CHUNK = 64
CONV_WIDTH = 31
HEAD_DIM = 64
LEFT_CHUNKS = 8
BAND = (LEFT_CHUNKS + 1) * CHUNK
MAX_REL = 256
N_EXPERTS = 64
TOP_K = 8
N_GROUPS = 8
GROUP_SIZE = N_EXPERTS // N_GROUPS
TOPK_GROUPS = 4
ROUTED_SCALE = 2.5
LN_EPS = 1e-5

LANES = 128
SUBLANES = 8
VMEM_LIMIT = 56 * 1024 * 1024

PAD_ROWS = LEFT_CHUNKS * CHUNK
HALO_ROWS = 32
CONV_ROW_CHUNK = 64
EXPERT_ROW_BLOCK = 256
NEG_BIG = -1e30

F32 = jnp.float32
BF16 = jnp.bfloat16
I32 = jnp.int32


def _cparams(semantics, vmem=VMEM_LIMIT):
    return pltpu.CompilerParams(dimension_semantics=semantics, vmem_limit_bytes=vmem)


def _layer_norm(x, g, b):
    mu = jnp.mean(x, axis=-1, keepdims=True)
    xc = x - mu
    var = jnp.mean(xc * xc, axis=-1, keepdims=True)
    return xc * lax.rsqrt(var + LN_EPS) * g + b


def _inproj_kernel(x_ref, g_ref, b_ref, w_ref, wvt_ref, h_ref, u_ref, vt_ref, hn_ref, *, apply_ln, n_u, tn):
    i = pl.program_id(1)
    j = pl.program_id(2)

    @pl.when(i == 0)
    def _():
        u_ref[...] = jnp.zeros_like(u_ref)
        vt_ref[...] = jnp.zeros_like(vt_ref)

    @pl.when((i > 0) & (j == 0))
    def _():
        x = x_ref[0]
        h = _layer_norm(x, g_ref[...], b_ref[...]) if apply_ln else x
        h_ref[0] = h
        hn_ref[...] = h.astype(BF16)

    @pl.when((i > 0) & (j < n_u))
    def _():
        cols = pl.ds(pl.multiple_of(j * tn, tn), tn)
        u_ref[0] = jnp.dot(hn_ref[...], w_ref[:, cols], preferred_element_type=F32).astype(BF16)

    @pl.when((i > 0) & (j == n_u))
    def _():
        vt = lax.dot_general(wvt_ref[...], hn_ref[...], (((1,), (1,)), ((), ())), preferred_element_type=F32)
        vt_ref[0] = vt.astype(BF16)


def _inproj(x, g, b, w_u, w_vt, *, apply_ln, tm, tn):
    bsz, s, d = x.shape
    width = w_u.shape[1]
    aw = w_vt.shape[0]
    assert PAD_ROWS % tm == 0 and s % tm == 0 and width % tn == 0
    pad_blocks = PAD_ROWS // tm
    n_u = width // tn
    grid = (bsz, s // tm + pad_blocks, n_u + 1)
    row = lambda bi, i, j: (bi, jnp.maximum(i - pad_blocks, 0), 0)
    const = lambda shape: pl.BlockSpec(shape, lambda bi, i, j: (0,) * len(shape), pipeline_mode=pl.Buffered(1))
    return pl.pallas_call(
        functools.partial(_inproj_kernel, apply_ln=apply_ln, n_u=n_u, tn=tn),
        grid=grid,
        in_specs=[
            pl.BlockSpec((1, tm, d), row),
            const((1, d)), const((1, d)),
            const((d, width)),
            const((aw, d)),
        ],
        out_specs=[
            pl.BlockSpec((1, tm, d), row),
            pl.BlockSpec((1, tm, tn), lambda bi, i, j: (bi, i, jnp.minimum(j, n_u - 1))),
            pl.BlockSpec((1, aw, tm), lambda bi, i, j: (bi, 0, i)),
        ],
        out_shape=[
            jax.ShapeDtypeStruct((bsz, s, d), F32),
            jax.ShapeDtypeStruct((bsz, s + PAD_ROWS, width), BF16),
            jax.ShapeDtypeStruct((bsz, aw, s + PAD_ROWS), BF16),
        ],
        scratch_shapes=[pltpu.VMEM((tm, d), BF16)],
        compiler_params=_cparams(("arbitrary", "arbitrary", "arbitrary")),
        name="inproj",
    )(x, g, b, w_u, w_vt)


def _conv_kernel(ah_ref, gh_ref, at_ref, gt_ref, w_ref, cb_ref, lg_ref, lb_ref, o_ref, v_scr, y_scr, *, ts, c):
    v_scr[0:HALO_ROWS, :] = ah_ref[0].astype(F32) * jax.nn.sigmoid(gh_ref[0].astype(F32))
    v_scr[HALO_ROWS:HALO_ROWS + ts, :] = at_ref[0].astype(F32) * jax.nn.sigmoid(gt_ref[0].astype(F32))
    first_tap = HALO_ROWS - (CONV_WIDTH - 1)
    win_rows = CONV_ROW_CHUNK + HALO_ROWS

    def row_chunk(rc, carry):
        r0 = pl.multiple_of(rc * CONV_ROW_CHUNK, CONV_ROW_CHUNK)
        for ci in range(c // LANES):
            lanes = slice(ci * LANES, (ci + 1) * LANES)
            win = v_scr[pl.ds(r0, win_rows), lanes]
            shifted = [win] + [pltpu.roll(win, win_rows - sh, 0) for sh in range(1, SUBLANES)]
            wc = w_ref[:, lanes]
            acc = jnp.zeros((CONV_ROW_CHUNK, LANES), F32)
            for k in range(CONV_WIDTH):
                start = (first_tap + k) // SUBLANES * SUBLANES
                acc = acc + wc[k:k + 1, :] * shifted[(first_tap + k) % SUBLANES][start:start + CONV_ROW_CHUNK, :]
            y_scr[pl.ds(r0, CONV_ROW_CHUNK), lanes] = acc + cb_ref[:, lanes]
        return carry

    lax.fori_loop(0, ts // CONV_ROW_CHUNK, row_chunk, 0)
    y = _layer_norm(y_scr[...], lg_ref[...], lb_ref[...])
    o_ref[0] = (y * jax.nn.sigmoid(y)).astype(BF16)


def _conv(u, conv_w, conv_b, ln_g, ln_b, *, s, c, ts):
    bsz = u.shape[0]
    assert PAD_ROWS % ts == 0 and s % ts == 0 and ts % HALO_ROWS == 0 and ts % CONV_ROW_CHUNK == 0
    hb = ts // HALO_ROWS
    pb = PAD_ROWS // ts
    halo = lambda col: pl.BlockSpec((1, HALO_ROWS, c), lambda bi, i: (bi, (pb + i) * hb - 1, col))
    tile = lambda col: pl.BlockSpec((1, ts, c), lambda bi, i: (bi, pb + i, col))
    vec = lambda rows: pl.BlockSpec((rows, c), lambda bi, i: (0, 0))
    return pl.pallas_call(
        functools.partial(_conv_kernel, ts=ts, c=c),
        grid=(bsz, s // ts),
        in_specs=[halo(0), halo(1), tile(0), tile(1), vec(CONV_WIDTH), vec(1), vec(1), vec(1)],
        out_specs=pl.BlockSpec((1, ts, c), lambda bi, i: (bi, i, 0)),
        out_shape=jax.ShapeDtypeStruct((bsz, s, c), BF16),
        scratch_shapes=[pltpu.VMEM((ts + HALO_ROWS, c), F32), pltpu.VMEM((ts, c), F32)],
        compiler_params=_cparams(("arbitrary", "arbitrary")),
        name="conv",
    )(u, u, u, u, conv_w, conv_b, ln_g, ln_b)


def _attn_kernel(q_ref, k0_ref, k1_ref, k2_ref, v0_ref, v1_ref, v2_ref, bias_ref, o_ref, kw_scr, vw_scr, st_scr, p_scr,
                 *, qb, aw):
    i = pl.program_id(1)
    for j, (kr, vr) in enumerate(((k0_ref, v0_ref), (k1_ref, v1_ref), (k2_ref, v2_ref))):
        kw_scr[j * qb:(j + 1) * qb, :] = kr[0]
        vw_scr[:, j * qb:(j + 1) * qb] = vr[0]
    lane = lax.broadcasted_iota(I32, (1, LANES), 1)
    first_head = lane < HEAD_DIM
    key_iota = lax.broadcasted_iota(I32, (BAND, LANES), 0)
    scale = jnp.asarray(HEAD_DIM ** -0.5, BF16)
    zero = jnp.zeros((), BF16)

    @pl.when((pl.program_id(0) == 0) & (i == 0))
    def _():
        p_scr[...] = jnp.zeros_like(p_scr)

    def run(masked):
        def head_pair(hp, carry):
            lo = pl.multiple_of(hp * LANES, LANES)
            qp = q_ref[0, :, pl.ds(lo, LANES)] * scale
            vtp = vw_scr[pl.ds(lo, LANES), :]
            bias = bias_ref[hp]
            n_chunks = qb // CHUNK
            for ca in range(n_chunks):
                qc = qp[ca * CHUNK:(ca + 1) * CHUNK]
                qs = jnp.concatenate([jnp.where(first_head, qc, zero), jnp.where(first_head, zero, qc)], axis=0)
                st = lax.dot_general(kw_scr[ca * CHUNK:ca * CHUNK + BAND, pl.ds(lo, LANES)], qs,
                                     (((1,), (1,)), ((), ())), preferred_element_type=F32) + bias
                if masked:
                    st = jnp.where(key_iota >= PAD_ROWS - (i * qb + ca * CHUNK), st, NEG_BIG)
                st_scr[ca] = st
            inv_l = []
            for ca in range(n_chunks):
                st = st_scr[ca]
                m = jnp.max(st, axis=0, keepdims=True)
                e = jnp.exp(st - m)
                inv_l.append(1.0 / jnp.sum(e, axis=0, keepdims=True))
                p_scr[ca, ca * CHUNK:ca * CHUNK + BAND, :] = e.astype(BF16)
            for ca in range(n_chunks):
                ot = jnp.dot(vtp, p_scr[ca], preferred_element_type=F32) * inv_l[ca]
                t = ot.T
                o_ref[0, ca * CHUNK:(ca + 1) * CHUNK, pl.ds(lo, LANES)] = jnp.where(
                    first_head, t[0:CHUNK], t[CHUNK:]).astype(BF16)
            return carry

        lax.fori_loop(0, aw // LANES, head_pair, 0)

    first_valid_block = PAD_ROWS // qb

    @pl.when(i < first_valid_block)
    def _():
        run(True)

    @pl.when(i >= first_valid_block)
    def _():
        run(False)


def _attention(u, vt, bias_t, *, s, aw, qb, qcol):
    bsz = u.shape[0]
    assert PAD_ROWS == 2 * qb and s % qb == 0 and aw % LANES == 0
    pb = PAD_ROWS // qb
    kspec = lambda j: pl.BlockSpec((1, qb, aw), lambda bi, i: (bi, i + j, qcol + 1))
    vspec = lambda j: pl.BlockSpec((1, aw, qb), lambda bi, i: (bi, 0, i + j))
    return pl.pallas_call(
        functools.partial(_attn_kernel, qb=qb, aw=aw),
        grid=(bsz, s // qb),
        in_specs=[pl.BlockSpec((1, qb, aw), lambda bi, i: (bi, pb + i, qcol))]
        + [kspec(j) for j in range(3)] + [vspec(j) for j in range(3)]
        + [pl.BlockSpec(bias_t.shape, lambda bi, i: (0, 0, 0))],
        out_specs=pl.BlockSpec((1, qb, aw), lambda bi, i: (bi, i, 0)),
        out_shape=jax.ShapeDtypeStruct((bsz, s, aw), BF16),
        scratch_shapes=[pltpu.VMEM((3 * qb, aw), BF16), pltpu.VMEM((aw, 3 * qb), BF16),
                        pltpu.VMEM((qb // CHUNK, BAND, LANES), F32), pltpu.VMEM((qb // CHUNK, 3 * qb, LANES), BF16)],
        compiler_params=_cparams(("arbitrary", "arbitrary")),
        name="attn",
    )(u, u, u, u, vt, vt, vt, bias_t)


def _expand_rel_bias(rel_bias):
    tail = jnp.broadcast_to(rel_bias[:, -1:], (rel_bias.shape[0], PAD_ROWS + CHUNK - 1 - MAX_REL))
    rev = jnp.concatenate([rel_bias, tail], axis=1)[:, ::-1].astype(F32)
    bias = jnp.stack([rev[:, CHUNK - 1 - i:CHUNK - 1 - i + BAND] for i in range(CHUNK)], axis=1)
    heads = bias.shape[0]
    return bias.reshape(heads // 2, 2, CHUNK, BAND).transpose(0, 3, 1, 2).reshape(heads // 2, BAND, 2 * CHUNK)


def _row_tiles(h):
    d = h.shape[1]
    chunks = jnp.stack([h[:, ci * LANES:(ci + 1) * LANES].astype(BF16) for ci in range(d // LANES)], axis=0)
    return jnp.swapaxes(chunks, 0, 1)


def _outproj_kernel(conv_ref, att_ref, h_ref, wo_ref, g_ref, b_ref, wr_ref, h1_ref, h1r_ref, lg_ref, *, alpha, c):
    mixed = jnp.dot(conv_ref[...], wo_ref[0:c, :], preferred_element_type=F32)
    mixed = mixed + jnp.dot(att_ref[...], wo_ref[c:, :], preferred_element_type=F32)
    h1 = _layer_norm(alpha * h_ref[...] + mixed, g_ref[...], b_ref[...])
    h1b = h1.astype(BF16)
    lg_ref[...] = lax.dot_general(wr_ref[...], h1b, (((1,), (1,)), ((), ())), preferred_element_type=F32)
    h1_ref[...] = h1
    h1r_ref[...] = _row_tiles(h1)


def _outproj(conv_out, att_out, h, wo, g, b, wr_t, *, alpha, tm):
    n, d = h.shape
    c = conv_out.shape[1]
    rpt = d // LANES
    const = lambda shape: pl.BlockSpec(shape, lambda i: (0,) * len(shape), pipeline_mode=pl.Buffered(1))
    return pl.pallas_call(
        functools.partial(_outproj_kernel, alpha=alpha, c=c),
        grid=(n // tm,),
        in_specs=[
            pl.BlockSpec((tm, c), lambda i: (i, 0)),
            pl.BlockSpec((tm, att_out.shape[1]), lambda i: (i, 0)),
            pl.BlockSpec((tm, d), lambda i: (i, 0)),
            const(wo.shape), const((1, d)), const((1, d)), const(wr_t.shape),
        ],
        out_specs=[
            pl.BlockSpec((tm, d), lambda i: (i, 0)),
            pl.BlockSpec((tm, rpt, LANES), lambda i: (i, 0, 0)),
            pl.BlockSpec((N_EXPERTS, tm), lambda i: (0, i)),
        ],
        out_shape=[
            jax.ShapeDtypeStruct((n, d), F32),
            jax.ShapeDtypeStruct((n, rpt, LANES), BF16),
            jax.ShapeDtypeStruct((N_EXPERTS, n), F32),
        ],
        compiler_params=_cparams(("arbitrary",)),
        name="outproj",
    )(conv_out, att_out, h, wo, g, b, wr_t)


def _router_kernel(lg_ref, rb_ref, tri_ref, pos_ref, gd_ref, cnt_ref, carry_scr, *, tt):
    i = pl.program_id(0)

    @pl.when(i == 0)
    def _():
        carry_scr[...] = jnp.zeros_like(carry_scr)

    scores = jax.nn.sigmoid(lg_ref[...])
    sel = scores + rb_ref[...]
    sel3 = sel.reshape(GROUP_SIZE, N_GROUPS, tt)
    jio = lax.broadcasted_iota(I32, (GROUP_SIZE, N_GROUPS, tt), 0)
    m1 = jnp.max(sel3, axis=0)
    first = jnp.min(jnp.where(sel3 == m1[None], jio, GROUP_SIZE), axis=0)
    m2 = jnp.max(jnp.where(jio == first[None], -jnp.inf, sel3), axis=0)
    gs = m1 + m2
    gio = lax.broadcasted_iota(I32, (N_GROUPS, tt), 0)
    grank = jnp.zeros((N_GROUPS, tt), I32)
    for gp in range(N_GROUPS):
        row = gs[gp:gp + 1, :]
        tie = jnp.where(gio > gp, 1, 0)
        grank = grank + jnp.where(row > gs, 1, jnp.where(row == gs, tie, 0))
    gmask = grank < TOPK_GROUPS
    masked = jnp.where(gmask[None], sel3, -jnp.inf).reshape(N_EXPERTS, tt)
    rio = lax.broadcasted_iota(I32, (N_EXPERTS, tt), 0)
    eid = (rio % N_GROUPS) * GROUP_SIZE + rio // N_GROUPS
    erank = jnp.zeros((N_EXPERTS, tt), I32)
    for rp in range(N_EXPERTS):
        ep = (rp % N_GROUPS) * GROUP_SIZE + rp // N_GROUPS
        row = masked[rp:rp + 1, :]
        tie = jnp.where(eid > ep, 1, 0)
        erank = erank + jnp.where(row > masked, 1, jnp.where(row == masked, tie, 0))
    smask = erank < TOP_K
    picked = jnp.where(smask, scores, 0.0)
    wsum = jnp.sum(picked, axis=0, keepdims=True)
    gd_ref[...] = picked / wsum * ROUTED_SCALE
    ones = jnp.where(smask, 1.0, 0.0)
    incl = jnp.dot(ones.astype(BF16), tri_ref[...], preferred_element_type=F32)
    pos = carry_scr[...] + incl - ones
    pos_ref[...] = jnp.where(smask, pos, -1.0).astype(I32)
    carry = carry_scr[...] + jnp.sum(ones, axis=1, keepdims=True)
    carry_scr[...] = carry
    cnt_ref[...] = jnp.broadcast_to(carry, (N_EXPERTS, LANES)).astype(I32)


def _router(logits_t, rb, tri, *, tt):
    n = logits_t.shape[1]
    tile = pl.BlockSpec((N_EXPERTS, tt), lambda i: (0, i))
    return pl.pallas_call(
        functools.partial(_router_kernel, tt=tt),
        grid=(n // tt,),
        in_specs=[tile, pl.BlockSpec((N_EXPERTS, 1), lambda i: (0, 0)), pl.BlockSpec((tt, tt), lambda i: (0, 0))],
        out_specs=[tile, tile, pl.BlockSpec((N_EXPERTS, LANES), lambda i: (0, 0))],
        out_shape=[
            jax.ShapeDtypeStruct((N_EXPERTS, n), I32),
            jax.ShapeDtypeStruct((N_EXPERTS, n), F32),
            jax.ShapeDtypeStruct((N_EXPERTS, LANES), I32),
        ],
        scratch_shapes=[pltpu.VMEM((N_EXPERTS, 1), F32)],
        compiler_params=_cparams(("arbitrary",)),
        name="router",
    )(logits_t, rb, tri)


def _slots_kernel(pos_ref, gd_ref, ps_ref, ltri_ref, dest_ref, gate_ref):
    pos = pos_ref[...]
    gd = gd_ref[...]
    chosen = pos >= 0
    ones = jnp.where(chosen, 1.0, 0.0).astype(BF16)
    before = jnp.dot(ltri_ref[...], ones, preferred_element_type=F32)
    dest_full = pos + ps_ref[...]
    for k in range(TOP_K):
        hit = jnp.where(chosen, before, -1.0) == float(k)
        dest_ref[k:k + 1, :] = jnp.sum(jnp.where(hit, dest_full, 0), axis=0, keepdims=True)
        gate_ref[k:k + 1, :] = jnp.sum(jnp.where(hit, gd, 0.0), axis=0, keepdims=True)


def _slots(pos, gd, pad_start, ltri, *, tt):
    n = pos.shape[1]
    tile = pl.BlockSpec((N_EXPERTS, tt), lambda i: (0, i))
    otile = pl.BlockSpec((TOP_K, tt), lambda i: (0, i))
    return pl.pallas_call(
        _slots_kernel,
        grid=(n // tt,),
        in_specs=[tile, tile, pl.BlockSpec((N_EXPERTS, 1), lambda i: (0, 0)),
                  pl.BlockSpec((N_EXPERTS, N_EXPERTS), lambda i: (0, 0))],
        out_specs=[otile, otile],
        out_shape=[jax.ShapeDtypeStruct((TOP_K, n), I32), jax.ShapeDtypeStruct((TOP_K, n), F32)],
        compiler_params=_cparams(("arbitrary",)),
        name="slots",
    )(pos, gd, pad_start, ltri)


def _dispatch_kernel(pad_end_ref, cnt_ref, dest_ref, rows_ref, h1_ref, wsg_ref, wsu_ref, wsd_ref, xs_ref, base_ref,
                     zero_scr, zsem, sem, *, td, rb, alpha):
    i = pl.program_id(0)

    def zero_copy(r):
        return pltpu.make_async_copy(zero_scr, xs_ref.at[pl.ds(pad_end_ref[r] - rb, rb)], zsem)

    @pl.when(i == 0)
    def _():
        zero_scr[...] = jnp.zeros_like(zero_scr)

        def start(r, carry):
            @pl.when(cnt_ref[r] > 0)
            def _():
                zero_copy(r).start()
            return carry

        def wait(r, carry):
            @pl.when(cnt_ref[r] > 0)
            def _():
                zero_copy(r).wait()
            return carry

        lax.fori_loop(0, N_EXPERTS, start, 0)
        lax.fori_loop(0, N_EXPERTS, wait, 0)

    def token(t, carry):
        src = rows_ref.at[t]
        for k in range(TOP_K):
            pltpu.make_async_copy(src, xs_ref.at[dest_ref[t * TOP_K + k]], sem).start()
        return carry

    lax.fori_loop(0, td, token, 0)

    h1 = h1_ref[...]
    h1b = h1.astype(BF16)
    sg = jnp.dot(h1b, wsg_ref[...], preferred_element_type=F32)
    su = jnp.dot(h1b, wsu_ref[...], preferred_element_type=F32)
    hid = (sg * jax.nn.sigmoid(sg) * su).astype(BF16)
    base_ref[...] = alpha * h1 + jnp.dot(hid, wsd_ref[...], preferred_element_type=F32)

    for k in range(TOP_K):
        pltpu.make_async_copy(rows_ref, xs_ref.at[pl.ds(0, td)], sem).wait()


def _dispatch(pad_end, counts, dest_flat, h1r, h1, wsg, wsu, wsd, *, n_rows, td, rb, alpha):
    n, rpt, _ = h1r.shape
    d = h1.shape[1]
    const = lambda shape: pl.BlockSpec(shape, lambda i, pe, cn: (0,) * len(shape), pipeline_mode=pl.Buffered(1))
    grid_spec = pltpu.PrefetchScalarGridSpec(
        num_scalar_prefetch=2,
        grid=(n // td,),
        in_specs=[
            pl.BlockSpec((td * TOP_K,), lambda i, pe, cn: (i,), memory_space=pltpu.SMEM),
            pl.BlockSpec((td, rpt, LANES), lambda i, pe, cn: (i, 0, 0)),
            pl.BlockSpec((td, d), lambda i, pe, cn: (i, 0)),
            const(wsg.shape), const(wsu.shape), const(wsd.shape),
        ],
        out_specs=[pl.BlockSpec(memory_space=pl.ANY), pl.BlockSpec((td, d), lambda i, pe, cn: (i, 0))],
        scratch_shapes=[pltpu.VMEM((rb, rpt, LANES), BF16), pltpu.SemaphoreType.DMA(()), pltpu.SemaphoreType.DMA(())],
    )
    return pl.pallas_call(
        functools.partial(_dispatch_kernel, td=td, rb=rb, alpha=alpha),
        grid_spec=grid_spec,
        out_shape=[jax.ShapeDtypeStruct((n_rows, rpt, LANES), BF16), jax.ShapeDtypeStruct((n, d), F32)],
        compiler_params=_cparams(("arbitrary",)),
        name="dispatch",
    )(pad_end, counts, dest_flat, h1r, h1, wsg, wsu, wsd)


def _expert_kernel(bidx_ref, bexp_ref, bnext_ref, nused_ref, xs_ref, wg_hbm, wu_hbm, wd_hbm, y_ref,
                   x_scr, wg_f32, wu_f32, wd_f32, wg_scr, wu_scr, wd_scr, wsem, *, rb, d):
    b = pl.program_id(0)

    def weight_copies(e):
        return (pltpu.make_async_copy(wg_hbm.at[e], wg_f32, wsem.at[0]),
                pltpu.make_async_copy(wu_hbm.at[e], wu_f32, wsem.at[1]),
                pltpu.make_async_copy(wd_hbm.at[e], wd_f32, wsem.at[2]))

    @pl.when(b == 0)
    def _():
        for cp in weight_copies(bexp_ref[0]):
            cp.start()

    new_expert = (b == 0) | (bexp_ref[b] != bexp_ref[jnp.maximum(b - 1, 0)])

    @pl.when(new_expert)
    def _():
        for cp in weight_copies(bexp_ref[b]):
            cp.wait()
        wg_scr[...] = wg_f32[...].astype(BF16)
        wu_scr[...] = wu_f32[...].astype(BF16)
        wd_scr[...] = wd_f32[...].astype(BF16)

        @pl.when(bnext_ref[b] >= 0)
        def _():
            for cp in weight_copies(bnext_ref[b]):
                cp.start()

    @pl.when(b < nused_ref[0])
    def _():
        xt = jnp.swapaxes(xs_ref[...], 0, 1)
        for ci in range(d // LANES):
            x_scr[:, ci * LANES:(ci + 1) * LANES] = xt[ci]
        x = x_scr[...]
        g = jnp.dot(x, wg_scr[...], preferred_element_type=F32)
        u = jnp.dot(x, wu_scr[...], preferred_element_type=F32)
        hid = (g * jax.nn.sigmoid(g) * u).astype(BF16)
        y = jnp.dot(hid, wd_scr[...], preferred_element_type=F32)
        y_ref[...] = _row_tiles(y)


def _experts(bidx, bexp, bnext, nused, xs, wg, wu, wd, *, rb, d):
    ypt = d // LANES
    n_blocks = xs.shape[0] // rb
    f = wg.shape[2]
    grid_spec = pltpu.PrefetchScalarGridSpec(
        num_scalar_prefetch=4,
        grid=(n_blocks,),
        in_specs=[
            pl.BlockSpec((rb, ypt, LANES), lambda b, bi, be, bn, nu: (bi[b], 0, 0)),
            pl.BlockSpec(memory_space=pl.ANY), pl.BlockSpec(memory_space=pl.ANY), pl.BlockSpec(memory_space=pl.ANY),
        ],
        out_specs=pl.BlockSpec((rb, ypt, LANES), lambda b, bi, be, bn, nu: (bi[b], 0, 0)),
        scratch_shapes=[pltpu.VMEM((rb, d), BF16),
                        pltpu.VMEM((d, f), F32), pltpu.VMEM((d, f), F32), pltpu.VMEM((f, d), F32),
                        pltpu.VMEM((d, f), BF16), pltpu.VMEM((d, f), BF16), pltpu.VMEM((f, d), BF16),
                        pltpu.SemaphoreType.DMA((3,))],
    )
    return pl.pallas_call(
        functools.partial(_expert_kernel, rb=rb, d=d),
        grid_spec=grid_spec,
        out_shape=jax.ShapeDtypeStruct((n_blocks * rb, ypt, LANES), BF16),
        compiler_params=_cparams(("arbitrary",)),
        name="experts",
    )(bidx, bexp, bnext, nused, xs, wg, wu, wd)


def _combine_kernel(dest_ref, gate_ref, base_ref, g_ref, b_ref, y_ref, o_ref, buf, ys_scr, r_scr, sem, *, tc, ypt, nt):
    i = pl.program_id(0)
    slot = i % 2

    @pl.when(i < nt)
    def _():
        def token(t, carry):
            for k in range(TOP_K):
                pltpu.make_async_copy(y_ref.at[dest_ref[t * TOP_K + k]], buf.at[slot * TOP_K + k, t],
                                      sem.at[slot]).start()
            return carry

        lax.fori_loop(0, tc, token, 0)

    @pl.when(i > 0)
    def _():
        ps = 1 - slot
        rows_like = buf.at[slot * TOP_K]
        for k in range(TOP_K):
            pltpu.make_async_copy(rows_like, buf.at[ps * TOP_K + k], sem.at[ps]).wait()
        gate = gate_ref[...]
        gk = [jnp.broadcast_to(gate[:, k:k + 1], (tc, LANES)) for k in range(TOP_K)]
        for k in range(TOP_K):
            yk = jnp.swapaxes(buf[ps * TOP_K + k], 0, 1)
            for ci in range(ypt):
                ys_scr[k, :, ci * LANES:(ci + 1) * LANES] = yk[ci]
        for ci in range(ypt):
            lanes = slice(ci * LANES, (ci + 1) * LANES)
            acc = base_ref[:, lanes]
            for k in range(TOP_K):
                acc = acc + gk[k] * ys_scr[k, :, lanes].astype(F32)
            r_scr[:, lanes] = acc
        o_ref[...] = _layer_norm(r_scr[...], g_ref[...], b_ref[...])


def _combine(dest_flat, gate_t, base, g, b, y, *, tc):
    n, d = base.shape
    ypt = d // LANES
    nt = n // tc
    prev = lambda i: (jnp.maximum(i - 1, 0), 0)
    return pl.pallas_call(
        functools.partial(_combine_kernel, tc=tc, ypt=ypt, nt=nt),
        grid=(nt + 1,),
        in_specs=[
            pl.BlockSpec((tc * TOP_K,), lambda i: (jnp.minimum(i, nt - 1),), memory_space=pltpu.SMEM),
            pl.BlockSpec((tc, TOP_K), prev),
            pl.BlockSpec((tc, d), prev),
            pl.BlockSpec((1, d), lambda i: (0, 0)),
            pl.BlockSpec((1, d), lambda i: (0, 0)),
            pl.BlockSpec(memory_space=pl.ANY),
        ],
        out_specs=pl.BlockSpec((tc, d), prev),
        out_shape=jax.ShapeDtypeStruct((n, d), F32),
        scratch_shapes=[
            pltpu.VMEM((2 * TOP_K, tc, ypt, LANES), BF16),
            pltpu.VMEM((TOP_K, tc, d), BF16),
            pltpu.VMEM((tc, d), F32),
            pltpu.SemaphoreType.DMA((2,)),
        ],
        compiler_params=_cparams(("arbitrary",)),
        name="combine",
    )(dest_flat, gate_t, base, g, b, y)


def _row_vec(v):
    return v.reshape(1, -1).astype(F32)


def _expert_rows(a):
    return a.reshape((N_GROUPS, GROUP_SIZE) + a.shape[1:]).swapaxes(0, 1).reshape(a.shape)


def _tile(n, want):
    t = min(n, want)
    assert n % t == 0
    return t


def kernel(x, ln_in_g, ln_in_b, w_in, conv_w, conv_b, conv_ln_g, conv_ln_b, rel_bias, w_out, ln1_g, ln1_b,
           w_router, router_bias, w_gate, w_up, w_down, w_shared_gate, w_shared_up, w_shared_down, ln2_g, ln2_b):
    bsz, s, d = x.shape
    n = bsz * s
    depth = w_in.shape[0]
    c = conv_w.shape[2]
    aw = rel_bias.shape[1] * HEAD_DIM
    assert c == aw and w_in.shape[2] == 2 * c + 3 * aw and w_router.shape[2] == N_EXPERTS
    alpha = (2 * depth) ** 0.25
    rb = EXPERT_ROW_BLOCK
    n_blocks = -(-n * TOP_K // rb) + N_EXPERTS
    n_rows = n_blocks * rb
    tt = _tile(n, 1024)
    tri = jnp.triu(jnp.ones((tt, tt), BF16))
    ltri = jnp.tril(jnp.ones((N_EXPERTS, N_EXPERTS), BF16), -1)

    h = x
    for li in range(depth):
        hn, u, vt = _inproj(h, _row_vec(ln_in_g), _row_vec(ln_in_b), w_in[li, :, :2 * c + 2 * aw].astype(BF16),
                            w_in[li, :, 2 * c + 2 * aw:].T.astype(BF16), apply_ln=(li == 0), tm=_tile(s, 512), tn=c)
        conv_out = _conv(u, conv_w[li].astype(F32), _row_vec(conv_b[li]), _row_vec(conv_ln_g[li]),
                         _row_vec(conv_ln_b[li]), s=s, c=c, ts=_tile(s, 512))
        att_out = _attention(u, vt, _expand_rel_bias(rel_bias[li]), s=s, aw=aw, qb=PAD_ROWS // 2, qcol=2)
        wr_t = _expert_rows(w_router[li].T).astype(BF16)
        h1, h1r, logits_t = _outproj(
            conv_out.reshape(n, c), att_out.reshape(n, aw), hn.reshape(n, d), w_out[li].astype(BF16),
            _row_vec(ln1_g[li]), _row_vec(ln1_b[li]), wr_t, alpha=alpha, tm=_tile(n, 512))
        rbias = _expert_rows(router_bias[li].reshape(N_EXPERTS, 1)).astype(F32)
        pos, gd, cnt = _router(logits_t, rbias, tri, tt=tt)
        counts = cnt[:, 0]
        padded = (counts + rb - 1) // rb * rb
        pad_end = jnp.cumsum(padded).astype(I32)
        pad_start = pad_end - padded
        dest, gate = _slots(pos, gd, pad_start.reshape(N_EXPERTS, 1), ltri, tt=tt)
        dest_flat = dest.T.reshape(n * TOP_K)
        xs, base = _dispatch(pad_end, counts, dest_flat, h1r, h1,
                             w_shared_gate[li].astype(BF16), w_shared_up[li].astype(BF16),
                             w_shared_down[li].astype(BF16), n_rows=n_rows, td=_tile(n, 256), rb=rb, alpha=alpha)
        n_used = pad_end[-1] // rb
        blk = jnp.minimum(jnp.arange(n_blocks, dtype=I32), n_used - 1)
        blk_row = jnp.sum((pad_end[None, :] <= (blk * rb)[:, None]).astype(I32), axis=1)
        row_ids = jnp.arange(N_EXPERTS, dtype=I32)
        expert_of_row = (row_ids % N_GROUPS) * GROUP_SIZE + row_ids // N_GROUPS
        candidate = (row_ids[None, :] > row_ids[:, None]) & (counts[None, :] > 0)
        next_row = jnp.min(jnp.where(candidate, row_ids[None, :], N_EXPERTS), axis=1)
        next_exp = jnp.where(next_row < N_EXPERTS, expert_of_row[jnp.minimum(next_row, N_EXPERTS - 1)], -1)
        y = _experts(blk, expert_of_row[blk_row], next_exp[blk_row], n_used.reshape(1), xs,
                     w_gate[li], w_up[li], w_down[li], rb=rb, d=d)
        h = _combine(dest_flat, gate.T, base, _row_vec(ln2_g[li]), _row_vec(ln2_b[li]), y,
                     tc=_tile(n, 128)).reshape(bsz, s, d)
    return h
```

```python
import functools

import jax
import jax.numpy as jnp
import numpy as np
from jax import lax
from jax.experimental import pallas as pl
from jax.experimental.pallas import tpu as pltpu

CHUNK = 64
CONV_WIDTH = 31
HEAD_DIM = 64
LEFT_CHUNKS = 8
BAND = (LEFT_CHUNKS + 1) * CHUNK
MAX_REL = 256
N_EXPERTS = 64
TOP_K = 8
N_GROUPS = 8
GROUP_SIZE = N_EXPERTS // N_GROUPS
TOPK_GROUPS = 4
ROUTED_SCALE = 2.5
LN_EPS = 1e-5

LANES = 128
SUBLANES = 8
VMEM_LIMIT = 56 * 1024 * 1024

PAD_ROWS = LEFT_CHUNKS * CHUNK
HALO_ROWS = 32
CONV_ROW_CHUNK = 64
EXPERT_ROW_BLOCK = 256
NEG_BIG = -1e30

F32 = jnp.float32
BF16 = jnp.bfloat16
I32 = jnp.int32


def _cparams(semantics, vmem=VMEM_LIMIT):
    return pltpu.CompilerParams(dimension_semantics=semantics, vmem_limit_bytes=vmem)


def _layer_norm(x, g, b):
    mu = jnp.mean(x, axis=-1, keepdims=True)
    xc = x - mu
    var = jnp.mean(xc * xc, axis=-1, keepdims=True)
    return xc * lax.rsqrt(var + LN_EPS) * g + b


def _inproj_kernel(x_ref, g_ref, b_ref, w_ref, wvt_ref, h_ref, u_ref, vt_ref, hn_ref, *, apply_ln, n_u, tn):
    i = pl.program_id(1)
    j = pl.program_id(2)

    @pl.when(i == 0)
    def _():
        u_ref[...] = jnp.zeros_like(u_ref)
        vt_ref[...] = jnp.zeros_like(vt_ref)

    @pl.when((i > 0) & (j == 0))
    def _():
        x = x_ref[0]
        h = _layer_norm(x, g_ref[...], b_ref[...]) if apply_ln else x
        h_ref[0] = h
        hn_ref[...] = h.astype(BF16)

    @pl.when((i > 0) & (j < n_u))
    def _():
        cols = pl.ds(pl.multiple_of(j * tn, tn), tn)
        u_ref[0] = jnp.dot(hn_ref[...], w_ref[:, cols], preferred_element_type=F32).astype(BF16)

    @pl.when((i > 0) & (j == n_u))
    def _():
        vt = lax.dot_general(wvt_ref[...], hn_ref[...], (((1,), (1,)), ((), ())), preferred_element_type=F32)
        vt_ref[0] = vt.astype(BF16)


def _inproj(x, g, b, w_u, w_vt, *, apply_ln, tm, tn):
    bsz, s, d = x.shape
    width = w_u.shape[1]
    aw = w_vt.shape[0]
    assert PAD_ROWS % tm == 0 and s % tm == 0 and width % tn == 0
    pad_blocks = PAD_ROWS // tm
    n_u = width // tn
    grid = (bsz, s // tm + pad_blocks, n_u + 1)
    row = lambda bi, i, j: (bi, jnp.maximum(i - pad_blocks, 0), 0)
    const = lambda shape: pl.BlockSpec(shape, lambda bi, i, j: (0,) * len(shape), pipeline_mode=pl.Buffered(1))
    return pl.pallas_call(
        functools.partial(_inproj_kernel, apply_ln=apply_ln, n_u=n_u, tn=tn),
        grid=grid,
        in_specs=[
            pl.BlockSpec((1, tm, d), row),
            const((1, d)), const((1, d)),
            const((d, width)),
            const((aw, d)),
        ],
        out_specs=[
            pl.BlockSpec((1, tm, d), row),
            pl.BlockSpec((1, tm, tn), lambda bi, i, j: (bi, i, jnp.minimum(j, n_u - 1))),
            pl.BlockSpec((1, aw, tm), lambda bi, i, j: (bi, 0, i)),
        ],
        out_shape=[
            jax.ShapeDtypeStruct((bsz, s, d), F32),
            jax.ShapeDtypeStruct((bsz, s + PAD_ROWS, width), BF16),
            jax.ShapeDtypeStruct((bsz, aw, s + PAD_ROWS), BF16),
        ],
        scratch_shapes=[pltpu.VMEM((tm, d), BF16)],
        compiler_params=_cparams(("arbitrary", "arbitrary", "arbitrary")),
        name="inproj",
    )(x, g, b, w_u, w_vt)


def _conv_kernel(ah_ref, gh_ref, at_ref, gt_ref, w_ref, cb_ref, lg_ref, lb_ref, o_ref, v_scr, y_scr, *, ts, c):
    v_scr[0:HALO_ROWS, :] = ah_ref[0].astype(F32) * jax.nn.sigmoid(gh_ref[0].astype(F32))
    v_scr[HALO_ROWS:HALO_ROWS + ts, :] = at_ref[0].astype(F32) * jax.nn.sigmoid(gt_ref[0].astype(F32))
    first_tap = HALO_ROWS - (CONV_WIDTH - 1)
    win_rows = CONV_ROW_CHUNK + HALO_ROWS

    def row_chunk(rc, carry):
        r0 = pl.multiple_of(rc * CONV_ROW_CHUNK, CONV_ROW_CHUNK)
        for ci in range(c // LANES):
            lanes = slice(ci * LANES, (ci + 1) * LANES)
            win = v_scr[pl.ds(r0, win_rows), lanes]
            shifted = [win] + [pltpu.roll(win, win_rows - sh, 0) for sh in range(1, SUBLANES)]
            wc = w_ref[:, lanes]
            acc = jnp.zeros((CONV_ROW_CHUNK, LANES), F32)
            for k in range(CONV_WIDTH):
                start = (first_tap + k) // SUBLANES * SUBLANES
                acc = acc + wc[k:k + 1, :] * shifted[(first_tap + k) % SUBLANES][start:start + CONV_ROW_CHUNK, :]
            y_scr[pl.ds(r0, CONV_ROW_CHUNK), lanes] = acc + cb_ref[:, lanes]
        return carry

    lax.fori_loop(0, ts // CONV_ROW_CHUNK, row_chunk, 0)
    y = _layer_norm(y_scr[...], lg_ref[...], lb_ref[...])
    o_ref[0] = (y * jax.nn.sigmoid(y)).astype(BF16)


def _conv(u, conv_w, conv_b, ln_g, ln_b, *, s, c, ts):
    bsz = u.shape[0]
    assert PAD_ROWS % ts == 0 and s % ts == 0 and ts % HALO_ROWS == 0 and ts % CONV_ROW_CHUNK == 0
    hb = ts // HALO_ROWS
    pb = PAD_ROWS // ts
    halo = lambda col: pl.BlockSpec((1, HALO_ROWS, c), lambda bi, i: (bi, (pb + i) * hb - 1, col))
    tile = lambda col: pl.BlockSpec((1, ts, c), lambda bi, i: (bi, pb + i, col))
    vec = lambda rows: pl.BlockSpec((rows, c), lambda bi, i: (0, 0))
    return pl.pallas_call(
        functools.partial(_conv_kernel, ts=ts, c=c),
        grid=(bsz, s // ts),
        in_specs=[halo(0), halo(1), tile(0), tile(1), vec(CONV_WIDTH), vec(1), vec(1), vec(1)],
        out_specs=pl.BlockSpec((1, ts, c), lambda bi, i: (bi, i, 0)),
        out_shape=jax.ShapeDtypeStruct((bsz, s, c), BF16),
        scratch_shapes=[pltpu.VMEM((ts + HALO_ROWS, c), F32), pltpu.VMEM((ts, c), F32)],
        compiler_params=_cparams(("arbitrary", "arbitrary")),
        name="conv",
    )(u, u, u, u, conv_w, conv_b, ln_g, ln_b)


def _attn_kernel(q_ref, k0_ref, k1_ref, k2_ref, v0_ref, v1_ref, v2_ref, bias_ref, o_ref, kw_scr, vw_scr, st_scr, p_scr,
                 *, qb, aw):
    i = pl.program_id(1)
    for j, (kr, vr) in enumerate(((k0_ref, v0_ref), (k1_ref, v1_ref), (k2_ref, v2_ref))):
        kw_scr[j * qb:(j + 1) * qb, :] = kr[0]
        vw_scr[:, j * qb:(j + 1) * qb] = vr[0]
    lane = lax.broadcasted_iota(I32, (1, LANES), 1)
    first_head = lane < HEAD_DIM
    key_iota = lax.broadcasted_iota(I32, (BAND, LANES), 0)
    scale = jnp.asarray(HEAD_DIM ** -0.5, BF16)
    zero = jnp.zeros((), BF16)

    @pl.when((pl.program_id(0) == 0) & (i == 0))
    def _():
        p_scr[...] = jnp.zeros_like(p_scr)

    def run(masked):
        def head_pair(hp, carry):
            lo = pl.multiple_of(hp * LANES, LANES)
            qp = q_ref[0, :, pl.ds(lo, LANES)] * scale
            vtp = vw_scr[pl.ds(lo, LANES), :]
            bias = bias_ref[hp]
            n_chunks = qb // CHUNK
            for ca in range(n_chunks):
                qc = qp[ca * CHUNK:(ca + 1) * CHUNK]
                qs = jnp.concatenate([jnp.where(first_head, qc, zero), jnp.where(first_head, zero, qc)], axis=0)
                st = lax.dot_general(kw_scr[ca * CHUNK:ca * CHUNK + BAND, pl.ds(lo, LANES)], qs,
                                     (((1,), (1,)), ((), ())), preferred_element_type=F32) + bias
                if masked:
                    st = jnp.where(key_iota >= PAD_ROWS - (i * qb + ca * CHUNK), st, NEG_BIG)
                st_scr[ca] = st
            inv_l = []
            for ca in range(n_chunks):
                st = st_scr[ca]
                m = jnp.max(st, axis=0, keepdims=True)
                e = jnp.exp(st - m)
                inv_l.append(1.0 / jnp.sum(e, axis=0, keepdims=True))
                p_scr[ca, ca * CHUNK:ca * CHUNK + BAND, :] = e.astype(BF16)
            for ca in range(n_chunks):
                ot = jnp.dot(vtp, p_scr[ca], preferred_element_type=F32) * inv_l[ca]
                t = ot.T
                o_ref[0, ca * CHUNK:(ca + 1) * CHUNK, pl.ds(lo, LANES)] = jnp.where(
                    first_head, t[0:CHUNK], t[CHUNK:]).astype(BF16)
            return carry

        lax.fori_loop(0, aw // LANES, head_pair, 0)

    first_valid_block = PAD_ROWS // qb

    @pl.when(i < first_valid_block)
    def _():
        run(True)

    @pl.when(i >= first_valid_block)
    def _():
        run(False)


def _attention(u, vt, bias_t, *, s, aw, qb, qcol):
    bsz = u.shape[0]
    assert PAD_ROWS == 2 * qb and s % qb == 0 and aw % LANES == 0
    pb = PAD_ROWS // qb
    kspec = lambda j: pl.BlockSpec((1, qb, aw), lambda bi, i: (bi, i + j, qcol + 1))
    vspec = lambda j: pl.BlockSpec((1, aw, qb), lambda bi, i: (bi, 0, i + j))
    return pl.pallas_call(
        functools.partial(_attn_kernel, qb=qb, aw=aw),
        grid=(bsz, s // qb),
        in_specs=[pl.BlockSpec((1, qb, aw), lambda bi, i: (bi, pb + i, qcol))]
        + [kspec(j) for j in range(3)] + [vspec(j) for j in range(3)]
        + [pl.BlockSpec(bias_t.shape, lambda bi, i: (0, 0, 0))],
        out_specs=pl.BlockSpec((1, qb, aw), lambda bi, i: (bi, i, 0)),
        out_shape=jax.ShapeDtypeStruct((bsz, s, aw), BF16),
        scratch_shapes=[pltpu.VMEM((3 * qb, aw), BF16), pltpu.VMEM((aw, 3 * qb), BF16),
                        pltpu.VMEM((qb // CHUNK, BAND, LANES), F32), pltpu.VMEM((qb // CHUNK, 3 * qb, LANES), BF16)],
        compiler_params=_cparams(("arbitrary", "arbitrary")),
        name="attn",
    )(u, u, u, u, vt, vt, vt, bias_t)


def _expand_rel_bias(rel_bias):
    tail = jnp.broadcast_to(rel_bias[:, -1:], (rel_bias.shape[0], PAD_ROWS + CHUNK - 1 - MAX_REL))
    rev = jnp.concatenate([rel_bias, tail], axis=1)[:, ::-1].astype(F32)
    bias = jnp.stack([rev[:, CHUNK - 1 - i:CHUNK - 1 - i + BAND] for i in range(CHUNK)], axis=1)
    heads = bias.shape[0]
    return bias.reshape(heads // 2, 2, CHUNK, BAND).transpose(0, 3, 1, 2).reshape(heads // 2, BAND, 2 * CHUNK)


def _row_tiles(h):
    d = h.shape[1]
    chunks = jnp.stack([h[:, ci * LANES:(ci + 1) * LANES].astype(BF16) for ci in range(d // LANES)], axis=0)
    return jnp.swapaxes(chunks, 0, 1)


def _outproj_kernel(conv_ref, att_ref, h_ref, wo_ref, g_ref, b_ref, wr_ref, h1_ref, h1r_ref, lg_ref, *, alpha, c):
    mixed = jnp.dot(conv_ref[...], wo_ref[0:c, :], preferred_element_type=F32)
    mixed = mixed + jnp.dot(att_ref[...], wo_ref[c:, :], preferred_element_type=F32)
    h1 = _layer_norm(alpha * h_ref[...] + mixed, g_ref[...], b_ref[...])
    h1b = h1.astype(BF16)
    lg_ref[...] = lax.dot_general(wr_ref[...], h1b, (((1,), (1,)), ((), ())), preferred_element_type=F32)
    h1_ref[...] = h1
    h1r_ref[...] = _row_tiles(h1)


def _outproj(conv_out, att_out, h, wo, g, b, wr_t, *, alpha, tm):
    n, d = h.shape
    c = conv_out.shape[1]
    rpt = d // LANES
    const = lambda shape: pl.BlockSpec(shape, lambda i: (0,) * len(shape), pipeline_mode=pl.Buffered(1))
    return pl.pallas_call(
        functools.partial(_outproj_kernel, alpha=alpha, c=c),
        grid=(n // tm,),
        in_specs=[
            pl.BlockSpec((tm, c), lambda i: (i, 0)),
            pl.BlockSpec((tm, att_out.shape[1]), lambda i: (i, 0)),
            pl.BlockSpec((tm, d), lambda i: (i, 0)),
            const(wo.shape), const((1, d)), const((1, d)), const(wr_t.shape),
        ],
        out_specs=[
            pl.BlockSpec((tm, d), lambda i: (i, 0)),
            pl.BlockSpec((tm, rpt, LANES), lambda i: (i, 0, 0)),
            pl.BlockSpec((N_EXPERTS, tm), lambda i: (0, i)),
        ],
        out_shape=[
            jax.ShapeDtypeStruct((n, d), F32),
            jax.ShapeDtypeStruct((n, rpt, LANES), BF16),
            jax.ShapeDtypeStruct((N_EXPERTS, n), F32),
        ],
        compiler_params=_cparams(("arbitrary",)),
        name="outproj",
    )(conv_out, att_out, h, wo, g, b, wr_t)


def _router_kernel(lg_ref, rb_ref, tri_ref, pos_ref, gd_ref, cnt_ref, carry_scr, *, tt):
    i = pl.program_id(0)

    @pl.when(i == 0)
    def _():
        carry_scr[...] = jnp.zeros_like(carry_scr)

    scores = jax.nn.sigmoid(lg_ref[...])
    sel = scores + rb_ref[...]
    sel3 = sel.reshape(GROUP_SIZE, N_GROUPS, tt)
    jio = lax.broadcasted_iota(I32, (GROUP_SIZE, N_GROUPS, tt), 0)
    m1 = jnp.max(sel3, axis=0)
    first = jnp.min(jnp.where(sel3 == m1[None], jio, GROUP_SIZE), axis=0)
    m2 = jnp.max(jnp.where(jio == first[None], -jnp.inf, sel3), axis=0)
    gs = m1 + m2
    gio = lax.broadcasted_iota(I32, (N_GROUPS, tt), 0)
    grank = jnp.zeros((N_GROUPS, tt), I32)
    for gp in range(N_GROUPS):
        row = gs[gp:gp + 1, :]
        tie = jnp.where(gio > gp, 1, 0)
        grank = grank + jnp.where(row > gs, 1, jnp.where(row == gs, tie, 0))
    gmask = grank < TOPK_GROUPS
    masked = jnp.where(gmask[None], sel3, -jnp.inf).reshape(N_EXPERTS, tt)
    rio = lax.broadcasted_iota(I32, (N_EXPERTS, tt), 0)
    eid = (rio % N_GROUPS) * GROUP_SIZE + rio // N_GROUPS
    erank = jnp.zeros((N_EXPERTS, tt), I32)
    for rp in range(N_EXPERTS):
        ep = (rp % N_GROUPS) * GROUP_SIZE + rp // N_GROUPS
        row = masked[rp:rp + 1, :]
        tie = jnp.where(eid > ep, 1, 0)
        erank = erank + jnp.where(row > masked, 1, jnp.where(row == masked, tie, 0))
    smask = erank < TOP_K
    picked = jnp.where(smask, scores, 0.0)
    wsum = jnp.sum(picked, axis=0, keepdims=True)
    gd_ref[...] = picked / wsum * ROUTED_SCALE
    ones = jnp.where(smask, 1.0, 0.0)
    incl = jnp.dot(ones.astype(BF16), tri_ref[...], preferred_element_type=F32)
    pos = carry_scr[...] + incl - ones
    pos_ref[...] = jnp.where(smask, pos, -1.0).astype(I32)
    carry = carry_scr[...] + jnp.sum(ones, axis=1, keepdims=True)
    carry_scr[...] = carry
    cnt_ref[...] = jnp.broadcast_to(carry, (N_EXPERTS, LANES)).astype(I32)


def _router(logits_t, rb, tri, *, tt):
    n = logits_t.shape[1]
    tile = pl.BlockSpec((N_EXPERTS, tt), lambda i: (0, i))
    return pl.pallas_call(
        functools.partial(_router_kernel, tt=tt),
        grid=(n // tt,),
        in_specs=[tile, pl.BlockSpec((N_EXPERTS, 1), lambda i: (0, 0)), pl.BlockSpec((tt, tt), lambda i: (0, 0))],
        out_specs=[tile, tile, pl.BlockSpec((N_EXPERTS, LANES), lambda i: (0, 0))],
        out_shape=[
            jax.ShapeDtypeStruct((N_EXPERTS, n), I32),
            jax.ShapeDtypeStruct((N_EXPERTS, n), F32),
            jax.ShapeDtypeStruct((N_EXPERTS, LANES), I32),
        ],
        scratch_shapes=[pltpu.VMEM((N_EXPERTS, 1), F32)],
        compiler_params=_cparams(("arbitrary",)),
        name="router",
    )(logits_t, rb, tri)


def _slots_kernel(pos_ref, gd_ref, ps_ref, ltri_ref, dest_ref, gate_ref):
    pos = pos_ref[...]
    gd = gd_ref[...]
    chosen = pos >= 0
    ones = jnp.where(chosen, 1.0, 0.0).astype(BF16)
    before = jnp.dot(ltri_ref[...], ones, preferred_element_type=F32)
    dest_full = pos + ps_ref[...]
    for k in range(TOP_K):
        hit = jnp.where(chosen, before, -1.0) == float(k)
        dest_ref[k:k + 1, :] = jnp.sum(jnp.where(hit, dest_full, 0), axis=0, keepdims=True)
        gate_ref[k:k + 1, :] = jnp.sum(jnp.where(hit, gd, 0.0), axis=0, keepdims=True)


def _slots(pos, gd, pad_start, ltri, *, tt):
    n = pos.shape[1]
    tile = pl.BlockSpec((N_EXPERTS, tt), lambda i: (0, i))
    otile = pl.BlockSpec((TOP_K, tt), lambda i: (0, i))
    return pl.pallas_call(
        _slots_kernel,
        grid=(n // tt,),
        in_specs=[tile, tile, pl.BlockSpec((N_EXPERTS, 1), lambda i: (0, 0)),
                  pl.BlockSpec((N_EXPERTS, N_EXPERTS), lambda i: (0, 0))],
        out_specs=[otile, otile],
        out_shape=[jax.ShapeDtypeStruct((TOP_K, n), I32), jax.ShapeDtypeStruct((TOP_K, n), F32)],
        compiler_params=_cparams(("arbitrary",)),
        name="slots",
    )(pos, gd, pad_start, ltri)


def _dispatch_kernel(pad_end_ref, cnt_ref, dest_ref, rows_ref, h1_ref, wsg_ref, wsu_ref, wsd_ref, xs_ref, base_ref,
                     zero_scr, zsem, sem, *, td, rb, alpha):
    i = pl.program_id(0)

    def zero_copy(r):
        return pltpu.make_async_copy(zero_scr, xs_ref.at[pl.ds(pad_end_ref[r] - rb, rb)], zsem)

    @pl.when(i == 0)
    def _():
        zero_scr[...] = jnp.zeros_like(zero_scr)

        def start(r, carry):
            @pl.when(cnt_ref[r] > 0)
            def _():
                zero_copy(r).start()
            return carry

        def wait(r, carry):
            @pl.when(cnt_ref[r] > 0)
            def _():
                zero_copy(r).wait()
            return carry

        lax.fori_loop(0, N_EXPERTS, start, 0)
        lax.fori_loop(0, N_EXPERTS, wait, 0)

    def token(t, carry):
        src = rows_ref.at[t]
        for k in range(TOP_K):
            pltpu.make_async_copy(src, xs_ref.at[dest_ref[t * TOP_K + k]], sem).start()
        return carry

    lax.fori_loop(0, td, token, 0)

    h1 = h1_ref[...]
    h1b = h1.astype(BF16)
    sg = jnp.dot(h1b, wsg_ref[...], preferred_element_type=F32)
    su = jnp.dot(h1b, wsu_ref[...], preferred_element_type=F32)
    hid = (sg * jax.nn.sigmoid(sg) * su).astype(BF16)
    base_ref[...] = alpha * h1 + jnp.dot(hid, wsd_ref[...], preferred_element_type=F32)

    for k in range(TOP_K):
        pltpu.make_async_copy(rows_ref, xs_ref.at[pl.ds(0, td)], sem).wait()


def _dispatch(pad_end, counts, dest_flat, h1r, h1, wsg, wsu, wsd, *, n_rows, td, rb, alpha):
    n, rpt, _ = h1r.shape
    d = h1.shape[1]
    const = lambda shape: pl.BlockSpec(shape, lambda i, pe, cn: (0,) * len(shape), pipeline_mode=pl.Buffered(1))
    grid_spec = pltpu.PrefetchScalarGridSpec(
        num_scalar_prefetch=2,
        grid=(n // td,),
        in_specs=[
            pl.BlockSpec((td * TOP_K,), lambda i, pe, cn: (i,), memory_space=pltpu.SMEM),
            pl.BlockSpec((td, rpt, LANES), lambda i, pe, cn: (i, 0, 0)),
            pl.BlockSpec((td, d), lambda i, pe, cn: (i, 0)),
            const(wsg.shape), const(wsu.shape), const(wsd.shape),
        ],
        out_specs=[pl.BlockSpec(memory_space=pl.ANY), pl.BlockSpec((td, d), lambda i, pe, cn: (i, 0))],
        scratch_shapes=[pltpu.VMEM((rb, rpt, LANES), BF16), pltpu.SemaphoreType.DMA(()), pltpu.SemaphoreType.DMA(())],
    )
    return pl.pallas_call(
        functools.partial(_dispatch_kernel, td=td, rb=rb, alpha=alpha),
        grid_spec=grid_spec,
        out_shape=[jax.ShapeDtypeStruct((n_rows, rpt, LANES), BF16), jax.ShapeDtypeStruct((n, d), F32)],
        compiler_params=_cparams(("arbitrary",)),
        name="dispatch",
    )(pad_end, counts, dest_flat, h1r, h1, wsg, wsu, wsd)


def _expert_kernel(bidx_ref, bexp_ref, bnext_ref, nused_ref, xs_ref, wg_hbm, wu_hbm, wd_hbm, y_ref,
                   x_scr, wg_f32, wu_f32, wd_f32, wg_scr, wu_scr, wd_scr, wsem, *, rb, d):
    b = pl.program_id(0)

    def weight_copies(e):
        return (pltpu.make_async_copy(wg_hbm.at[e], wg_f32, wsem.at[0]),
                pltpu.make_async_copy(wu_hbm.at[e], wu_f32, wsem.at[1]),
                pltpu.make_async_copy(wd_hbm.at[e], wd_f32, wsem.at[2]))

    @pl.when(b == 0)
    def _():
        for cp in weight_copies(bexp_ref[0]):
            cp.start()

    new_expert = (b == 0) | (bexp_ref[b] != bexp_ref[jnp.maximum(b - 1, 0)])

    @pl.when(new_expert)
    def _():
        for cp in weight_copies(bexp_ref[b]):
            cp.wait()
        wg_scr[...] = wg_f32[...].astype(BF16)
        wu_scr[...] = wu_f32[...].astype(BF16)
        wd_scr[...] = wd_f32[...].astype(BF16)

        @pl.when(bnext_ref[b] >= 0)
        def _():
            for cp in weight_copies(bnext_ref[b]):
                cp.start()

    @pl.when(b < nused_ref[0])
    def _():
        xt = jnp.swapaxes(xs_ref[...], 0, 1)
        for ci in range(d // LANES):
            x_scr[:, ci * LANES:(ci + 1) * LANES] = xt[ci]
        x = x_scr[...]
        g = jnp.dot(x, wg_scr[...], preferred_element_type=F32)
        u = jnp.dot(x, wu_scr[...], preferred_element_type=F32)
        hid = (g * jax.nn.sigmoid(g) * u).astype(BF16)
        y = jnp.dot(hid, wd_scr[...], preferred_element_type=F32)
        y_ref[...] = _row_tiles(y)


def _experts(bidx, bexp, bnext, nused, xs, wg, wu, wd, *, rb, d):
    ypt = d // LANES
    n_blocks = xs.shape[0] // rb
    f = wg.shape[2]
    grid_spec = pltpu.PrefetchScalarGridSpec(
        num_scalar_prefetch=4,
        grid=(n_blocks,),
        in_specs=[
            pl.BlockSpec((rb, ypt, LANES), lambda b, bi, be, bn, nu: (bi[b], 0, 0)),
            pl.BlockSpec(memory_space=pl.ANY), pl.BlockSpec(memory_space=pl.ANY), pl.BlockSpec(memory_space=pl.ANY),
        ],
        out_specs=pl.BlockSpec((rb, ypt, LANES), lambda b, bi, be, bn, nu: (bi[b], 0, 0)),
        scratch_shapes=[pltpu.VMEM((rb, d), BF16),
                        pltpu.VMEM((d, f), F32), pltpu.VMEM((d, f), F32), pltpu.VMEM((f, d), F32),
                        pltpu.VMEM((d, f), BF16), pltpu.VMEM((d, f), BF16), pltpu.VMEM((f, d), BF16),
                        pltpu.SemaphoreType.DMA((3,))],
    )
    return pl.pallas_call(
        functools.partial(_expert_kernel, rb=rb, d=d),
        grid_spec=grid_spec,
        out_shape=jax.ShapeDtypeStruct((n_blocks * rb, ypt, LANES), BF16),
        compiler_params=_cparams(("arbitrary",)),
        name="experts",
    )(bidx, bexp, bnext, nused, xs, wg, wu, wd)


def _combine_kernel(dest_ref, gate_ref, base_ref, g_ref, b_ref, y_ref, o_ref, buf, ys_scr, r_scr, sem, *, tc, ypt, nt):
    i = pl.program_id(0)
    slot = i % 2

    @pl.when(i < nt)
    def _():
        def token(t, carry):
            for k in range(TOP_K):
                pltpu.make_async_copy(y_ref.at[dest_ref[t * TOP_K + k]], buf.at[slot * TOP_K + k, t],
                                      sem.at[slot]).start()
            return carry

        lax.fori_loop(0, tc, token, 0)

    @pl.when(i > 0)
    def _():
        ps = 1 - slot
        rows_like = buf.at[slot * TOP_K]
        for k in range(TOP_K):
            pltpu.make_async_copy(rows_like, buf.at[ps * TOP_K + k], sem.at[ps]).wait()
        gate = gate_ref[...]
        gk = [jnp.broadcast_to(gate[:, k:k + 1], (tc, LANES)) for k in range(TOP_K)]
        for k in range(TOP_K):
            yk = jnp.swapaxes(buf[ps * TOP_K + k], 0, 1)
            for ci in range(ypt):
                ys_scr[k, :, ci * LANES:(ci + 1) * LANES] = yk[ci]
        for ci in range(ypt):
            lanes = slice(ci * LANES, (ci + 1) * LANES)
            acc = base_ref[:, lanes]
            for k in range(TOP_K):
                acc = acc + gk[k] * ys_scr[k, :, lanes].astype(F32)
            r_scr[:, lanes] = acc
        o_ref[...] = _layer_norm(r_scr[...], g_ref[...], b_ref[...])


def _combine(dest_flat, gate_t, base, g, b, y, *, tc):
    n, d = base.shape
    ypt = d // LANES
    nt = n // tc
    prev = lambda i: (jnp.maximum(i - 1, 0), 0)
    return pl.pallas_call(
        functools.partial(_combine_kernel, tc=tc, ypt=ypt, nt=nt),
        grid=(nt + 1,),
        in_specs=[
            pl.BlockSpec((tc * TOP_K,), lambda i: (jnp.minimum(i, nt - 1),), memory_space=pltpu.SMEM),
            pl.BlockSpec((tc, TOP_K), prev),
            pl.BlockSpec((tc, d), prev),
            pl.BlockSpec((1, d), lambda i: (0, 0)),
            pl.BlockSpec((1, d), lambda i: (0, 0)),
            pl.BlockSpec(memory_space=pl.ANY),
        ],
        out_specs=pl.BlockSpec((tc, d), prev),
        out_shape=jax.ShapeDtypeStruct((n, d), F32),
        scratch_shapes=[
            pltpu.VMEM((2 * TOP_K, tc, ypt, LANES), BF16),
            pltpu.VMEM((TOP_K, tc, d), BF16),
            pltpu.VMEM((tc, d), F32),
            pltpu.SemaphoreType.DMA((2,)),
        ],
        compiler_params=_cparams(("arbitrary",)),
        name="combine",
    )(dest_flat, gate_t, base, g, b, y)


def _row_vec(v):
    return v.reshape(1, -1).astype(F32)


def _expert_rows(a):
    return a.reshape((N_GROUPS, GROUP_SIZE) + a.shape[1:]).swapaxes(0, 1).reshape(a.shape)


def _tile(n, want):
    t = min(n, want)
    assert n % t == 0
    return t


def kernel(x, ln_in_g, ln_in_b, w_in, conv_w, conv_b, conv_ln_g, conv_ln_b, rel_bias, w_out, ln1_g, ln1_b,
           w_router, router_bias, w_gate, w_up, w_down, w_shared_gate, w_shared_up, w_shared_down, ln2_g, ln2_b):
    bsz, s, d = x.shape
    n = bsz * s
    depth = w_in.shape[0]
    c = conv_w.shape[2]
    aw = rel_bias.shape[1] * HEAD_DIM
    assert c == aw and w_in.shape[2] == 2 * c + 3 * aw and w_router.shape[2] == N_EXPERTS
    alpha = (2 * depth) ** 0.25
    rb = EXPERT_ROW_BLOCK
    n_blocks = -(-n * TOP_K // rb) + N_EXPERTS
    n_rows = n_blocks * rb
    tt = _tile(n, 1024)
    tri = jnp.asarray(np.triu(np.ones((tt, tt), np.float32)), BF16)
    ltri = jnp.asarray(np.tril(np.ones((N_EXPERTS, N_EXPERTS), np.float32), -1), BF16)

    h = x
    for li in range(depth):
        hn, u, vt = _inproj(h, _row_vec(ln_in_g), _row_vec(ln_in_b), w_in[li, :, :2 * c + 2 * aw].astype(BF16),
                            w_in[li, :, 2 * c + 2 * aw:].T.astype(BF16), apply_ln=(li == 0), tm=_tile(s, 512), tn=c)
        conv_out = _conv(u, conv_w[li].astype(F32), _row_vec(conv_b[li]), _row_vec(conv_ln_g[li]),
                         _row_vec(conv_ln_b[li]), s=s, c=c, ts=_tile(s, 512))
        att_out = _attention(u, vt, _expand_rel_bias(rel_bias[li]), s=s, aw=aw, qb=PAD_ROWS // 2, qcol=2)
        wr_t = _expert_rows(w_router[li].T).astype(BF16)
        h1, h1r, logits_t = _outproj(
            conv_out.reshape(n, c), att_out.reshape(n, aw), hn.reshape(n, d), w_out[li].astype(BF16),
            _row_vec(ln1_g[li]), _row_vec(ln1_b[li]), wr_t, alpha=alpha, tm=_tile(n, 512))
        rbias = _expert_rows(router_bias[li].reshape(N_EXPERTS, 1)).astype(F32)
        pos, gd, cnt = _router(logits_t, rbias, tri, tt=tt)
        counts = cnt[:, 0]
        padded = (counts + rb - 1) // rb * rb
        pad_end = jnp.cumsum(padded).astype(I32)
        pad_start = pad_end - padded
        dest, gate = _slots(pos, gd, pad_start.reshape(N_EXPERTS, 1), ltri, tt=tt)
        dest_flat = dest.T.reshape(n * TOP_K)
        xs, base = _dispatch(pad_end, counts, dest_flat, h1r, h1,
                             w_shared_gate[li].astype(BF16), w_shared_up[li].astype(BF16),
                             w_shared_down[li].astype(BF16), n_rows=n_rows, td=_tile(n, 256), rb=rb, alpha=alpha)
        n_used = pad_end[-1] // rb
        blk = jnp.minimum(jnp.arange(n_blocks, dtype=I32), n_used - 1)
        blk_row = jnp.sum((pad_end[None, :] <= (blk * rb)[:, None]).astype(I32), axis=1)
        expert_of_row = lambda r: (r % N_GROUPS) * GROUP_SIZE + r // N_GROUPS
        row_ids = jnp.arange(N_EXPERTS, dtype=I32)
        candidate = (row_ids[None, :] > blk_row[:, None]) & (counts[None, :] > 0)
        next_row = jnp.min(jnp.where(candidate, row_ids[None, :], N_EXPERTS), axis=1)
        blk_next = jnp.where(next_row < N_EXPERTS, expert_of_row(next_row), -1)
        y = _experts(blk, expert_of_row(blk_row), blk_next, n_used.reshape(1), xs,
                     w_gate[li], w_up[li], w_down[li], rb=rb, d=d)
        h = _combine(dest_flat, gate.T, base, _row_vec(ln2_g[li]), _row_vec(ln2_b[li]), y,
                     tc=_tile(n, 128)).reshape(bsz, s, d)
    return h
```

```python
import functools

import jax
import jax.numpy as jnp
import numpy as np
from jax import lax
from jax.experimental import pallas as pl
from jax.experimental.pallas import tpu as pltpu

CHUNK = 64
CONV_WIDTH = 31
HEAD_DIM = 64
LEFT_CHUNKS = 8
BAND = (LEFT_CHUNKS + 1) * CHUNK
MAX_REL = 256
N_EXPERTS = 64
TOP_K = 8
N_GROUPS = 8
GROUP_SIZE = N_EXPERTS // N_GROUPS
TOPK_GROUPS = 4
ROUTED_SCALE = 2.5
LN_EPS = 1e-5

LANES = 128
SUBLANES = 8
VMEM_LIMIT = 56 * 1024 * 1024

PAD_ROWS = LEFT_CHUNKS * CHUNK
HALO_ROWS = 32
CONV_ROW_CHUNK = 64
ATTN_PAIRS_PER_TRIP = 4
EXPERT_ROW_BLOCK = 256
NEG_BIG = -1e30

F32 = jnp.float32
BF16 = jnp.bfloat16
I32 = jnp.int32


def _cparams(semantics, vmem=VMEM_LIMIT):
    return pltpu.CompilerParams(dimension_semantics=semantics, vmem_limit_bytes=vmem)


def _layer_norm(x, g, b):
    mu = jnp.mean(x, axis=-1, keepdims=True)
    xc = x - mu
    var = jnp.mean(xc * xc, axis=-1, keepdims=True)
    return xc * lax.rsqrt(var + LN_EPS) * g + b


def _inproj_kernel(x_ref, g_ref, b_ref, w_ref, wvt_ref, h_ref, u_ref, vt_ref, hn_ref, *, apply_ln, n_u, tn):
    i = pl.program_id(1)
    j = pl.program_id(2)

    @pl.when(i == 0)
    def _():
        u_ref[...] = jnp.zeros_like(u_ref)
        vt_ref[...] = jnp.zeros_like(vt_ref)

    @pl.when((i > 0) & (j == 0))
    def _():
        x = x_ref[0]
        h = _layer_norm(x, g_ref[...], b_ref[...]) if apply_ln else x
        h_ref[0] = h
        hn_ref[...] = h.astype(BF16)

    @pl.when((i > 0) & (j < n_u))
    def _():
        cols = pl.ds(pl.multiple_of(j * tn, tn), tn)
        u_ref[0] = jnp.dot(hn_ref[...], w_ref[:, cols], preferred_element_type=F32).astype(BF16)

    @pl.when((i > 0) & (j == n_u))
    def _():
        vt = lax.dot_general(wvt_ref[...], hn_ref[...], (((1,), (1,)), ((), ())), preferred_element_type=F32)
        vt_ref[0] = vt.astype(BF16)


def _inproj(x, g, b, w_u, w_vt, *, apply_ln, tm, tn):
    bsz, s, d = x.shape
    width = w_u.shape[1]
    aw = w_vt.shape[0]
    assert PAD_ROWS % tm == 0 and s % tm == 0 and width % tn == 0
    pad_blocks = PAD_ROWS // tm
    n_u = width // tn
    grid = (bsz, s // tm + pad_blocks, n_u + 1)
    row = lambda bi, i, j: (bi, jnp.maximum(i - pad_blocks, 0), 0)
    const = lambda shape: pl.BlockSpec(shape, lambda bi, i, j: (0,) * len(shape), pipeline_mode=pl.Buffered(1))
    return pl.pallas_call(
        functools.partial(_inproj_kernel, apply_ln=apply_ln, n_u=n_u, tn=tn),
        grid=grid,
        in_specs=[
            pl.BlockSpec((1, tm, d), row),
            const((1, d)), const((1, d)),
            const((d, width)),
            const((aw, d)),
        ],
        out_specs=[
            pl.BlockSpec((1, tm, d), row),
            pl.BlockSpec((1, tm, tn), lambda bi, i, j: (bi, i, jnp.minimum(j, n_u - 1))),
            pl.BlockSpec((1, aw, tm), lambda bi, i, j: (bi, 0, i)),
        ],
        out_shape=[
            jax.ShapeDtypeStruct((bsz, s, d), F32),
            jax.ShapeDtypeStruct((bsz, s + PAD_ROWS, width), BF16),
            jax.ShapeDtypeStruct((bsz, aw, s + PAD_ROWS), BF16),
        ],
        scratch_shapes=[pltpu.VMEM((tm, d), BF16)],
        compiler_params=_cparams(("arbitrary", "arbitrary", "arbitrary")),
        name="inproj",
    )(x, g, b, w_u, w_vt)


def _conv_kernel(ah_ref, gh_ref, at_ref, gt_ref, w_ref, cb_ref, lg_ref, lb_ref, o_ref, v_scr, y_scr, *, ts, c):
    v_scr[0:HALO_ROWS, :] = ah_ref[0].astype(F32) * jax.nn.sigmoid(gh_ref[0].astype(F32))
    v_scr[HALO_ROWS:HALO_ROWS + ts, :] = at_ref[0].astype(F32) * jax.nn.sigmoid(gt_ref[0].astype(F32))
    first_tap = HALO_ROWS - (CONV_WIDTH - 1)
    win_rows = CONV_ROW_CHUNK + HALO_ROWS

    def row_chunk(rc, carry):
        r0 = pl.multiple_of(rc * CONV_ROW_CHUNK, CONV_ROW_CHUNK)
        for ci in range(c // LANES):
            lanes = slice(ci * LANES, (ci + 1) * LANES)
            win = v_scr[pl.ds(r0, win_rows), lanes]
            shifted = [win] + [pltpu.roll(win, win_rows - sh, 0) for sh in range(1, SUBLANES)]
            wc = w_ref[:, lanes]
            acc = jnp.zeros((CONV_ROW_CHUNK, LANES), F32)
            for k in range(CONV_WIDTH):
                start = (first_tap + k) // SUBLANES * SUBLANES
                acc = acc + wc[k:k + 1, :] * shifted[(first_tap + k) % SUBLANES][start:start + CONV_ROW_CHUNK, :]
            y_scr[pl.ds(r0, CONV_ROW_CHUNK), lanes] = acc + cb_ref[:, lanes]
        return carry

    lax.fori_loop(0, ts // CONV_ROW_CHUNK, row_chunk, 0)
    y = _layer_norm(y_scr[...], lg_ref[...], lb_ref[...])
    o_ref[0] = (y * jax.nn.sigmoid(y)).astype(BF16)


def _conv(u, conv_w, conv_b, ln_g, ln_b, *, s, c, ts):
    bsz = u.shape[0]
    assert PAD_ROWS % ts == 0 and s % ts == 0 and ts % HALO_ROWS == 0 and ts % CONV_ROW_CHUNK == 0
    hb = ts // HALO_ROWS
    pb = PAD_ROWS // ts
    halo = lambda col: pl.BlockSpec((1, HALO_ROWS, c), lambda bi, i: (bi, (pb + i) * hb - 1, col))
    tile = lambda col: pl.BlockSpec((1, ts, c), lambda bi, i: (bi, pb + i, col))
    vec = lambda rows: pl.BlockSpec((rows, c), lambda bi, i: (0, 0))
    return pl.pallas_call(
        functools.partial(_conv_kernel, ts=ts, c=c),
        grid=(bsz, s // ts),
        in_specs=[halo(0), halo(1), tile(0), tile(1), vec(CONV_WIDTH), vec(1), vec(1), vec(1)],
        out_specs=pl.BlockSpec((1, ts, c), lambda bi, i: (bi, i, 0)),
        out_shape=jax.ShapeDtypeStruct((bsz, s, c), BF16),
        scratch_shapes=[pltpu.VMEM((ts + HALO_ROWS, c), F32), pltpu.VMEM((ts, c), F32)],
        compiler_params=_cparams(("arbitrary", "arbitrary")),
        name="conv",
    )(u, u, u, u, conv_w, conv_b, ln_g, ln_b)


def _attn_kernel(q_ref, k0_ref, k1_ref, k2_ref, v0_ref, v1_ref, v2_ref, bias_ref, o_ref, kw_scr, vw_scr, st_scr, p_scr,
                 *, qb, aw):
    i = pl.program_id(1)
    for j, (kr, vr) in enumerate(((k0_ref, v0_ref), (k1_ref, v1_ref), (k2_ref, v2_ref))):
        kw_scr[j * qb:(j + 1) * qb, :] = kr[0]
        vw_scr[:, j * qb:(j + 1) * qb] = vr[0]
    lane = lax.broadcasted_iota(I32, (1, LANES), 1)
    first_head = lane < HEAD_DIM
    key_iota = lax.broadcasted_iota(I32, (BAND, LANES), 0)
    scale = jnp.asarray(HEAD_DIM ** -0.5, BF16)
    zero = jnp.zeros((), BF16)

    @pl.when((pl.program_id(0) == 0) & (i == 0))
    def _():
        p_scr[...] = jnp.zeros_like(p_scr)

    def run(masked):
        def head_pairs(hq, carry):
            n_chunks = qb // CHUNK
            los = [pl.multiple_of((ATTN_PAIRS_PER_TRIP * hq + par) * LANES, LANES) for par in range(ATTN_PAIRS_PER_TRIP)]
            vtps = [vw_scr[pl.ds(lo, LANES), :] for lo in los]
            for par, lo in enumerate(los):
                qp = q_ref[0, :, pl.ds(lo, LANES)] * scale
                bias = bias_ref[ATTN_PAIRS_PER_TRIP * hq + par]
                for ca in range(n_chunks):
                    qc = qp[ca * CHUNK:(ca + 1) * CHUNK]
                    qs = jnp.concatenate([jnp.where(first_head, qc, zero), jnp.where(first_head, zero, qc)], axis=0)
                    st = lax.dot_general(kw_scr[ca * CHUNK:ca * CHUNK + BAND, pl.ds(lo, LANES)], qs,
                                         (((1,), (1,)), ((), ())), preferred_element_type=F32) + bias
                    if masked:
                        st = jnp.where(key_iota >= PAD_ROWS - (i * qb + ca * CHUNK), st, NEG_BIG)
                    st_scr[par * n_chunks + ca] = st
            inv_l = []
            for u in range(ATTN_PAIRS_PER_TRIP * n_chunks):
                ca = u % n_chunks
                st = st_scr[u]
                m = jnp.max(st, axis=0, keepdims=True)
                e = jnp.exp(st - m)
                inv_l.append(1.0 / jnp.sum(e, axis=0, keepdims=True))
                p_scr[u, ca * CHUNK:ca * CHUNK + BAND, :] = e.astype(BF16)
            for u in range(ATTN_PAIRS_PER_TRIP * n_chunks):
                par, ca = divmod(u, n_chunks)
                ot = jnp.dot(vtps[par], p_scr[u], preferred_element_type=F32) * inv_l[u]
                t = ot.T
                o_ref[0, ca * CHUNK:(ca + 1) * CHUNK, pl.ds(los[par], LANES)] = jnp.where(
                    first_head, t[0:CHUNK], t[CHUNK:]).astype(BF16)
            return carry

        lax.fori_loop(0, aw // LANES // ATTN_PAIRS_PER_TRIP, head_pairs, 0)

    first_valid_block = PAD_ROWS // qb

    @pl.when(i < first_valid_block)
    def _():
        run(True)

    @pl.when(i >= first_valid_block)
    def _():
        run(False)


def _attention(u, vt, bias_t, *, s, aw, qb, qcol):
    bsz = u.shape[0]
    assert PAD_ROWS == 2 * qb and s % qb == 0 and aw % LANES == 0
    pb = PAD_ROWS // qb
    kspec = lambda j: pl.BlockSpec((1, qb, aw), lambda bi, i: (bi, i + j, qcol + 1))
    vspec = lambda j: pl.BlockSpec((1, aw, qb), lambda bi, i: (bi, 0, i + j))
    return pl.pallas_call(
        functools.partial(_attn_kernel, qb=qb, aw=aw),
        grid=(bsz, s // qb),
        in_specs=[pl.BlockSpec((1, qb, aw), lambda bi, i: (bi, pb + i, qcol))]
        + [kspec(j) for j in range(3)] + [vspec(j) for j in range(3)]
        + [pl.BlockSpec(bias_t.shape, lambda bi, i: (0, 0, 0))],
        out_specs=pl.BlockSpec((1, qb, aw), lambda bi, i: (bi, i, 0)),
        out_shape=jax.ShapeDtypeStruct((bsz, s, aw), BF16),
        scratch_shapes=[pltpu.VMEM((3 * qb, aw), BF16), pltpu.VMEM((aw, 3 * qb), BF16),
                        pltpu.VMEM((ATTN_PAIRS_PER_TRIP * qb // CHUNK, BAND, LANES), F32),
                        pltpu.VMEM((ATTN_PAIRS_PER_TRIP * qb // CHUNK, 3 * qb, LANES), BF16)],
        compiler_params=_cparams(("arbitrary", "arbitrary")),
        name="attn",
    )(u, u, u, u, vt, vt, vt, bias_t)


def _expand_rel_bias(rel_bias):
    tail = jnp.broadcast_to(rel_bias[:, -1:], (rel_bias.shape[0], PAD_ROWS + CHUNK - 1 - MAX_REL))
    rev = jnp.concatenate([rel_bias, tail], axis=1)[:, ::-1].astype(F32)
    bias = jnp.stack([rev[:, CHUNK - 1 - i:CHUNK - 1 - i + BAND] for i in range(CHUNK)], axis=1)
    heads = bias.shape[0]
    return bias.reshape(heads // 2, 2, CHUNK, BAND).transpose(0, 3, 1, 2).reshape(heads // 2, BAND, 2 * CHUNK)


def _row_tiles(h):
    d = h.shape[1]
    chunks = jnp.stack([h[:, ci * LANES:(ci + 1) * LANES].astype(BF16) for ci in range(d // LANES)], axis=0)
    return jnp.swapaxes(chunks, 0, 1)


def _outproj_kernel(conv_ref, att_ref, h_ref, wo_ref, g_ref, b_ref, wr_ref, h1_ref, h1r_ref, lg_ref, *, alpha, c):
    mixed = jnp.dot(conv_ref[...], wo_ref[0:c, :], preferred_element_type=F32)
    mixed = mixed + jnp.dot(att_ref[...], wo_ref[c:, :], preferred_element_type=F32)
    h1 = _layer_norm(alpha * h_ref[...] + mixed, g_ref[...], b_ref[...])
    h1b = h1.astype(BF16)
    lg_ref[...] = lax.dot_general(wr_ref[...], h1b, (((1,), (1,)), ((), ())), preferred_element_type=F32)
    h1_ref[...] = h1
    h1r_ref[...] = _row_tiles(h1)


def _outproj(conv_out, att_out, h, wo, g, b, wr_t, *, alpha, tm):
    n, d = h.shape
    c = conv_out.shape[1]
    rpt = d // LANES
    const = lambda shape: pl.BlockSpec(shape, lambda i: (0,) * len(shape), pipeline_mode=pl.Buffered(1))
    return pl.pallas_call(
        functools.partial(_outproj_kernel, alpha=alpha, c=c),
        grid=(n // tm,),
        in_specs=[
            pl.BlockSpec((tm, c), lambda i: (i, 0)),
            pl.BlockSpec((tm, att_out.shape[1]), lambda i: (i, 0)),
            pl.BlockSpec((tm, d), lambda i: (i, 0)),
            const(wo.shape), const((1, d)), const((1, d)), const(wr_t.shape),
        ],
        out_specs=[
            pl.BlockSpec((tm, d), lambda i: (i, 0)),
            pl.BlockSpec((tm, rpt, LANES), lambda i: (i, 0, 0)),
            pl.BlockSpec((N_EXPERTS, tm), lambda i: (0, i)),
        ],
        out_shape=[
            jax.ShapeDtypeStruct((n, d), F32),
            jax.ShapeDtypeStruct((n, rpt, LANES), BF16),
            jax.ShapeDtypeStruct((N_EXPERTS, n), F32),
        ],
        compiler_params=_cparams(("arbitrary",)),
        name="outproj",
    )(conv_out, att_out, h, wo, g, b, wr_t)


def _router_kernel(lg_ref, rb_ref, tri_ref, pos_ref, gd_ref, cnt_ref, carry_scr, *, tt):
    i = pl.program_id(0)

    @pl.when(i == 0)
    def _():
        carry_scr[...] = jnp.zeros_like(carry_scr)

    scores = jax.nn.sigmoid(lg_ref[...])
    sel = scores + rb_ref[...]
    sel3 = sel.reshape(GROUP_SIZE, N_GROUPS, tt)
    jio = lax.broadcasted_iota(I32, (GROUP_SIZE, N_GROUPS, tt), 0)
    m1 = jnp.max(sel3, axis=0)
    first = jnp.min(jnp.where(sel3 == m1[None], jio, GROUP_SIZE), axis=0)
    m2 = jnp.max(jnp.where(jio == first[None], -jnp.inf, sel3), axis=0)
    gs = m1 + m2
    gio = lax.broadcasted_iota(I32, (N_GROUPS, tt), 0)
    grank = jnp.zeros((N_GROUPS, tt), I32)
    for gp in range(N_GROUPS):
        row = gs[gp:gp + 1, :]
        tie = jnp.where(gio > gp, 1, 0)
        grank = grank + jnp.where(row > gs, 1, jnp.where(row == gs, tie, 0))
    gmask = grank < TOPK_GROUPS
    masked = jnp.where(gmask[None], sel3, -jnp.inf).reshape(N_EXPERTS, tt)
    rio = lax.broadcasted_iota(I32, (N_EXPERTS, tt), 0)
    eid = (rio % N_GROUPS) * GROUP_SIZE + rio // N_GROUPS
    erank = jnp.zeros((N_EXPERTS, tt), I32)
    for rp in range(N_EXPERTS):
        ep = (rp % N_GROUPS) * GROUP_SIZE + rp // N_GROUPS
        row = masked[rp:rp + 1, :]
        tie = jnp.where(eid > ep, 1, 0)
        erank = erank + jnp.where(row > masked, 1, jnp.where(row == masked, tie, 0))
    smask = erank < TOP_K
    picked = jnp.where(smask, scores, 0.0)
    wsum = jnp.sum(picked, axis=0, keepdims=True)
    gd_ref[...] = picked / wsum * ROUTED_SCALE
    ones = jnp.where(smask, 1.0, 0.0)
    incl = jnp.dot(ones.astype(BF16), tri_ref[...], preferred_element_type=F32)
    pos = carry_scr[...] + incl - ones
    pos_ref[...] = jnp.where(smask, pos, -1.0).astype(I32)
    carry = carry_scr[...] + jnp.sum(ones, axis=1, keepdims=True)
    carry_scr[...] = carry
    cnt_ref[...] = jnp.broadcast_to(carry, (N_EXPERTS, LANES)).astype(I32)


def _router(logits_t, rb, tri, *, tt):
    n = logits_t.shape[1]
    tile = pl.BlockSpec((N_EXPERTS, tt), lambda i: (0, i))
    return pl.pallas_call(
        functools.partial(_router_kernel, tt=tt),
        grid=(n // tt,),
        in_specs=[tile, pl.BlockSpec((N_EXPERTS, 1), lambda i: (0, 0)), pl.BlockSpec((tt, tt), lambda i: (0, 0))],
        out_specs=[tile, tile, pl.BlockSpec((N_EXPERTS, LANES), lambda i: (0, 0))],
        out_shape=[
            jax.ShapeDtypeStruct((N_EXPERTS, n), I32),
            jax.ShapeDtypeStruct((N_EXPERTS, n), F32),
            jax.ShapeDtypeStruct((N_EXPERTS, LANES), I32),
        ],
        scratch_shapes=[pltpu.VMEM((N_EXPERTS, 1), F32)],
        compiler_params=_cparams(("arbitrary",)),
        name="router",
    )(logits_t, rb, tri)


def _slots_kernel(pos_ref, gd_ref, ps_ref, ltri_ref, dest_ref, gate_ref):
    pos = pos_ref[...]
    gd = gd_ref[...]
    chosen = pos >= 0
    ones = jnp.where(chosen, 1.0, 0.0).astype(BF16)
    before = jnp.dot(ltri_ref[...], ones, preferred_element_type=F32)
    dest_full = pos + ps_ref[...]
    for k in range(TOP_K):
        hit = jnp.where(chosen, before, -1.0) == float(k)
        dest_ref[k:k + 1, :] = jnp.sum(jnp.where(hit, dest_full, 0), axis=0, keepdims=True)
        gate_ref[k:k + 1, :] = jnp.sum(jnp.where(hit, gd, 0.0), axis=0, keepdims=True)


def _slots(pos, gd, pad_start, ltri, *, tt):
    n = pos.shape[1]
    tile = pl.BlockSpec((N_EXPERTS, tt), lambda i: (0, i))
    otile = pl.BlockSpec((TOP_K, tt), lambda i: (0, i))
    return pl.pallas_call(
        _slots_kernel,
        grid=(n // tt,),
        in_specs=[tile, tile, pl.BlockSpec((N_EXPERTS, 1), lambda i: (0, 0)),
                  pl.BlockSpec((N_EXPERTS, N_EXPERTS), lambda i: (0, 0))],
        out_specs=[otile, otile],
        out_shape=[jax.ShapeDtypeStruct((TOP_K, n), I32), jax.ShapeDtypeStruct((TOP_K, n), F32)],
        compiler_params=_cparams(("arbitrary",)),
        name="slots",
    )(pos, gd, pad_start, ltri)


def _dispatch_kernel(pad_end_ref, cnt_ref, dest_ref, rows_ref, h1_ref, wsg_ref, wsu_ref, wsd_ref, xs_ref, base_ref,
                     zero_scr, zsem, sem, *, td, rb, alpha):
    i = pl.program_id(0)

    def zero_copy(r):
        return pltpu.make_async_copy(zero_scr, xs_ref.at[pl.ds(pad_end_ref[r] - rb, rb)], zsem)

    @pl.when(i == 0)
    def _():
        zero_scr[...] = jnp.zeros_like(zero_scr)

        def start(r, carry):
            @pl.when(cnt_ref[r] > 0)
            def _():
                zero_copy(r).start()
            return carry

        def wait(r, carry):
            @pl.when(cnt_ref[r] > 0)
            def _():
                zero_copy(r).wait()
            return carry

        lax.fori_loop(0, N_EXPERTS, start, 0)
        lax.fori_loop(0, N_EXPERTS, wait, 0)

    def token(t, carry):
        src = rows_ref.at[t]
        for k in range(TOP_K):
            pltpu.make_async_copy(src, xs_ref.at[dest_ref[t * TOP_K + k]], sem).start()
        return carry

    lax.fori_loop(0, td, token, 0)

    h1 = h1_ref[...]
    h1b = h1.astype(BF16)
    sg = jnp.dot(h1b, wsg_ref[...], preferred_element_type=F32)
    su = jnp.dot(h1b, wsu_ref[...], preferred_element_type=F32)
    hid = (sg * jax.nn.sigmoid(sg) * su).astype(BF16)
    base_ref[...] = alpha * h1 + jnp.dot(hid, wsd_ref[...], preferred_element_type=F32)

    for k in range(TOP_K):
        pltpu.make_async_copy(rows_ref, xs_ref.at[pl.ds(0, td)], sem).wait()


def _dispatch(pad_end, counts, dest_flat, h1r, h1, wsg, wsu, wsd, *, n_rows, td, rb, alpha):
    n, rpt, _ = h1r.shape
    d = h1.shape[1]
    const = lambda shape: pl.BlockSpec(shape, lambda i, pe, cn: (0,) * len(shape), pipeline_mode=pl.Buffered(1))
    grid_spec = pltpu.PrefetchScalarGridSpec(
        num_scalar_prefetch=2,
        grid=(n // td,),
        in_specs=[
            pl.BlockSpec((td * TOP_K,), lambda i, pe, cn: (i,), memory_space=pltpu.SMEM),
            pl.BlockSpec((td, rpt, LANES), lambda i, pe, cn: (i, 0, 0)),
            pl.BlockSpec((td, d), lambda i, pe, cn: (i, 0)),
            const(wsg.shape), const(wsu.shape), const(wsd.shape),
        ],
        out_specs=[pl.BlockSpec(memory_space=pl.ANY), pl.BlockSpec((td, d), lambda i, pe, cn: (i, 0))],
        scratch_shapes=[pltpu.VMEM((rb, rpt, LANES), BF16), pltpu.SemaphoreType.DMA(()), pltpu.SemaphoreType.DMA(())],
    )
    return pl.pallas_call(
        functools.partial(_dispatch_kernel, td=td, rb=rb, alpha=alpha),
        grid_spec=grid_spec,
        out_shape=[jax.ShapeDtypeStruct((n_rows, rpt, LANES), BF16), jax.ShapeDtypeStruct((n, d), F32)],
        compiler_params=_cparams(("arbitrary",)),
        name="dispatch",
    )(pad_end, counts, dest_flat, h1r, h1, wsg, wsu, wsd)


def _expert_kernel(bidx_ref, bexp_ref, bnext_ref, nused_ref, xs_ref, wg_hbm, wu_hbm, wd_hbm, y_ref,
                   x_scr, wg_f32, wu_f32, wd_f32, wg_scr, wu_scr, wd_scr, wsem, *, rb, d):
    b = pl.program_id(0)

    def weight_copies(e):
        return (pltpu.make_async_copy(wg_hbm.at[e], wg_f32, wsem.at[0]),
                pltpu.make_async_copy(wu_hbm.at[e], wu_f32, wsem.at[1]),
                pltpu.make_async_copy(wd_hbm.at[e], wd_f32, wsem.at[2]))

    @pl.when(b == 0)
    def _():
        for cp in weight_copies(bexp_ref[0]):
            cp.start()

    new_expert = (b == 0) | (bexp_ref[b] != bexp_ref[jnp.maximum(b - 1, 0)])

    @pl.when(new_expert)
    def _():
        for cp in weight_copies(bexp_ref[b]):
            cp.wait()
        wg_scr[...] = wg_f32[...].astype(BF16)
        wu_scr[...] = wu_f32[...].astype(BF16)
        wd_scr[...] = wd_f32[...].astype(BF16)

        @pl.when(bnext_ref[b] >= 0)
        def _():
            for cp in weight_copies(bnext_ref[b]):
                cp.start()

    @pl.when(b < nused_ref[0])
    def _():
        xt = jnp.swapaxes(xs_ref[...], 0, 1)
        for ci in range(d // LANES):
            x_scr[:, ci * LANES:(ci + 1) * LANES] = xt[ci]
        x = x_scr[...]
        g = jnp.dot(x, wg_scr[...], preferred_element_type=F32)
        u = jnp.dot(x, wu_scr[...], preferred_element_type=F32)
        hid = (g * jax.nn.sigmoid(g) * u).astype(BF16)
        y = jnp.dot(hid, wd_scr[...], preferred_element_type=F32)
        y_ref[...] = _row_tiles(y)


def _experts(bidx, bexp, bnext, nused, xs, wg, wu, wd, *, rb, d):
    ypt = d // LANES
    n_blocks = xs.shape[0] // rb
    f = wg.shape[2]
    grid_spec = pltpu.PrefetchScalarGridSpec(
        num_scalar_prefetch=4,
        grid=(n_blocks,),
        in_specs=[
            pl.BlockSpec((rb, ypt, LANES), lambda b, bi, be, bn, nu: (bi[b], 0, 0)),
            pl.BlockSpec(memory_space=pl.ANY), pl.BlockSpec(memory_space=pl.ANY), pl.BlockSpec(memory_space=pl.ANY),
        ],
        out_specs=pl.BlockSpec((rb, ypt, LANES), lambda b, bi, be, bn, nu: (bi[b], 0, 0)),
        scratch_shapes=[pltpu.VMEM((rb, d), BF16),
                        pltpu.VMEM((d, f), F32), pltpu.VMEM((d, f), F32), pltpu.VMEM((f, d), F32),
                        pltpu.VMEM((d, f), BF16), pltpu.VMEM((d, f), BF16), pltpu.VMEM((f, d), BF16),
                        pltpu.SemaphoreType.DMA((3,))],
    )
    return pl.pallas_call(
        functools.partial(_expert_kernel, rb=rb, d=d),
        grid_spec=grid_spec,
        out_shape=jax.ShapeDtypeStruct((n_blocks * rb, ypt, LANES), BF16),
        compiler_params=_cparams(("arbitrary",)),
        name="experts",
    )(bidx, bexp, bnext, nused, xs, wg, wu, wd)


def _combine_kernel(dest_ref, gate_ref, base_ref, g_ref, b_ref, y_ref, o_ref, buf, ys_scr, r_scr, sem, *, tc, ypt, nt):
    i = pl.program_id(0)
    slot = i % 2

    @pl.when(i < nt)
    def _():
        def token(t, carry):
            for k in range(TOP_K):
                pltpu.make_async_copy(y_ref.at[dest_ref[t * TOP_K + k]], buf.at[slot * TOP_K + k, t],
                                      sem.at[slot]).start()
            return carry

        lax.fori_loop(0, tc, token, 0)

    @pl.when(i > 0)
    def _():
        ps = 1 - slot
        rows_like = buf.at[slot * TOP_K]
        for k in range(TOP_K):
            pltpu.make_async_copy(rows_like, buf.at[ps * TOP_K + k], sem.at[ps]).wait()
        gate = gate_ref[...]
        gk = [jnp.broadcast_to(gate[:, k:k + 1], (tc, LANES)) for k in range(TOP_K)]
        for k in range(TOP_K):
            yk = jnp.swapaxes(buf[ps * TOP_K + k], 0, 1)
            for ci in range(ypt):
                ys_scr[k, :, ci * LANES:(ci + 1) * LANES] = yk[ci]
        for ci in range(ypt):
            lanes = slice(ci * LANES, (ci + 1) * LANES)
            acc = base_ref[:, lanes]
            for k in range(TOP_K):
                acc = acc + gk[k] * ys_scr[k, :, lanes].astype(F32)
            r_scr[:, lanes] = acc
        o_ref[...] = _layer_norm(r_scr[...], g_ref[...], b_ref[...])


def _combine(dest_flat, gate_t, base, g, b, y, *, tc):
    n, d = base.shape
    ypt = d // LANES
    nt = n // tc
    prev = lambda i: (jnp.maximum(i - 1, 0), 0)
    return pl.pallas_call(
        functools.partial(_combine_kernel, tc=tc, ypt=ypt, nt=nt),
        grid=(nt + 1,),
        in_specs=[
            pl.BlockSpec((tc * TOP_K,), lambda i: (jnp.minimum(i, nt - 1),), memory_space=pltpu.SMEM),
            pl.BlockSpec((tc, TOP_K), prev),
            pl.BlockSpec((tc, d), prev),
            pl.BlockSpec((1, d), lambda i: (0, 0)),
            pl.BlockSpec((1, d), lambda i: (0, 0)),
            pl.BlockSpec(memory_space=pl.ANY),
        ],
        out_specs=pl.BlockSpec((tc, d), prev),
        out_shape=jax.ShapeDtypeStruct((n, d), F32),
        scratch_shapes=[
            pltpu.VMEM((2 * TOP_K, tc, ypt, LANES), BF16),
            pltpu.VMEM((TOP_K, tc, d), BF16),
            pltpu.VMEM((tc, d), F32),
            pltpu.SemaphoreType.DMA((2,)),
        ],
        compiler_params=_cparams(("arbitrary",)),
        name="combine",
    )(dest_flat, gate_t, base, g, b, y)


def _row_vec(v):
    return v.reshape(1, -1).astype(F32)


def _expert_rows(a):
    return a.reshape((N_GROUPS, GROUP_SIZE) + a.shape[1:]).swapaxes(0, 1).reshape(a.shape)


def _tile(n, want):
    t = min(n, want)
    assert n % t == 0
    return t


def kernel(x, ln_in_g, ln_in_b, w_in, conv_w, conv_b, conv_ln_g, conv_ln_b, rel_bias, w_out, ln1_g, ln1_b,
           w_router, router_bias, w_gate, w_up, w_down, w_shared_gate, w_shared_up, w_shared_down, ln2_g, ln2_b):
    bsz, s, d = x.shape
    n = bsz * s
    depth = w_in.shape[0]
    c = conv_w.shape[2]
    aw = rel_bias.shape[1] * HEAD_DIM
    assert c == aw and w_in.shape[2] == 2 * c + 3 * aw and w_router.shape[2] == N_EXPERTS
    alpha = (2 * depth) ** 0.25
    rb = EXPERT_ROW_BLOCK
    n_blocks = -(-n * TOP_K // rb) + N_EXPERTS
    n_rows = n_blocks * rb
    tt = _tile(n, 1024)
    tri = jnp.asarray(np.triu(np.ones((tt, tt), np.float32)), BF16)
    ltri = jnp.asarray(np.tril(np.ones((N_EXPERTS, N_EXPERTS), np.float32), -1), BF16)

    h = x
    for li in range(depth):
        hn, u, vt = _inproj(h, _row_vec(ln_in_g), _row_vec(ln_in_b), w_in[li, :, :2 * c + 2 * aw].astype(BF16),
                            w_in[li, :, 2 * c + 2 * aw:].T.astype(BF16), apply_ln=(li == 0), tm=_tile(s, 512), tn=c)
        conv_out = _conv(u, conv_w[li].astype(F32), _row_vec(conv_b[li]), _row_vec(conv_ln_g[li]),
                         _row_vec(conv_ln_b[li]), s=s, c=c, ts=_tile(s, 512))
        att_out = _attention(u, vt, _expand_rel_bias(rel_bias[li]), s=s, aw=aw, qb=PAD_ROWS // 2, qcol=2)
        wr_t = _expert_rows(w_router[li].T).astype(BF16)
        h1, h1r, logits_t = _outproj(
            conv_out.reshape(n, c), att_out.reshape(n, aw), hn.reshape(n, d), w_out[li].astype(BF16),
            _row_vec(ln1_g[li]), _row_vec(ln1_b[li]), wr_t, alpha=alpha, tm=_tile(n, 512))
        rbias = _expert_rows(router_bias[li].reshape(N_EXPERTS, 1)).astype(F32)
        pos, gd, cnt = _router(logits_t, rbias, tri, tt=tt)
        counts = cnt[:, 0]
        padded = (counts + rb - 1) // rb * rb
        pad_end = jnp.cumsum(padded).astype(I32)
        pad_start = pad_end - padded
        dest, gate = _slots(pos, gd, pad_start.reshape(N_EXPERTS, 1), ltri, tt=tt)
        dest_flat = dest.T.reshape(n * TOP_K)
        xs, base = _dispatch(pad_end, counts, dest_flat, h1r, h1,
                             w_shared_gate[li].astype(BF16), w_shared_up[li].astype(BF16),
                             w_shared_down[li].astype(BF16), n_rows=n_rows, td=_tile(n, 256), rb=rb, alpha=alpha)
        n_used = pad_end[-1] // rb
        blk = jnp.minimum(jnp.arange(n_blocks, dtype=I32), n_used - 1)
        blk_row = jnp.sum((pad_end[None, :] <= (blk * rb)[:, None]).astype(I32), axis=1)
        expert_of_row = lambda r: (r % N_GROUPS) * GROUP_SIZE + r // N_GROUPS
        row_ids = jnp.arange(N_EXPERTS, dtype=I32)
        candidate = (row_ids[None, :] > blk_row[:, None]) & (counts[None, :] > 0)
        next_row = jnp.min(jnp.where(candidate, row_ids[None, :], N_EXPERTS), axis=1)
        blk_next = jnp.where(next_row < N_EXPERTS, expert_of_row(next_row), -1)
        y = _experts(blk, expert_of_row(blk_row), blk_next, n_used.reshape(1), xs,
                     w_gate[li], w_up[li], w_down[li], rb=rb, d=d)
        h = _combine(dest_flat, gate.T, base, _row_vec(ln2_g[li]), _row_vec(ln2_b[li]), y,
                     tc=_tile(n, 128)).reshape(bsz, s, d)
    return h
```

```python
import functools

import jax
import jax.numpy as jnp
import numpy as np
from jax import lax
from jax.experimental import pallas as pl
from jax.experimental.pallas import tpu as pltpu

CHUNK = 64
CONV_WIDTH = 31
HEAD_DIM = 64
LEFT_CHUNKS = 8
BAND = (LEFT_CHUNKS + 1) * CHUNK
MAX_REL = 256
N_EXPERTS = 64
TOP_K = 8
N_GROUPS = 8
GROUP_SIZE = N_EXPERTS // N_GROUPS
TOPK_GROUPS = 4
ROUTED_SCALE = 2.5
LN_EPS = 1e-5

LANES = 128
SUBLANES = 8
VMEM_LIMIT = 56 * 1024 * 1024

PAD_ROWS = LEFT_CHUNKS * CHUNK
HALO_ROWS = 32
CONV_ROW_CHUNK = 64
ATTN_PAIRS_PER_TRIP = 4
EXPERT_ROW_BLOCK = 256
NEG_BIG = -1e30

F32 = jnp.float32
BF16 = jnp.bfloat16
I32 = jnp.int32


def _cparams(semantics, vmem=VMEM_LIMIT):
    return pltpu.CompilerParams(dimension_semantics=semantics, vmem_limit_bytes=vmem)


def _layer_norm(x, g, b):
    mu = jnp.mean(x, axis=-1, keepdims=True)
    xc = x - mu
    var = jnp.mean(xc * xc, axis=-1, keepdims=True)
    return xc * lax.rsqrt(var + LN_EPS) * g + b


def _inproj_kernel(x_ref, g_ref, b_ref, w_ref, wvt_ref, h_ref, u_ref, vt_ref, hn_ref, *, apply_ln, n_u, tn):
    i = pl.program_id(1)
    j = pl.program_id(2)

    @pl.when(i == 0)
    def _():
        u_ref[...] = jnp.zeros_like(u_ref)
        vt_ref[...] = jnp.zeros_like(vt_ref)

    @pl.when((i > 0) & (j == 0))
    def _():
        x = x_ref[0]
        h = _layer_norm(x, g_ref[...], b_ref[...]) if apply_ln else x
        h_ref[0] = h
        hn_ref[...] = h.astype(BF16)

    @pl.when((i > 0) & (j < n_u))
    def _():
        cols = pl.ds(pl.multiple_of(j * tn, tn), tn)
        u_ref[0] = jnp.dot(hn_ref[...], w_ref[:, cols], preferred_element_type=F32).astype(BF16)

    @pl.when((i > 0) & (j == n_u))
    def _():
        vt = lax.dot_general(wvt_ref[...], hn_ref[...], (((1,), (1,)), ((), ())), preferred_element_type=F32)
        vt_ref[0] = vt.astype(BF16)


def _inproj(x, g, b, w_u, w_vt, *, apply_ln, tm, tn):
    bsz, s, d = x.shape
    width = w_u.shape[1]
    aw = w_vt.shape[0]
    assert PAD_ROWS % tm == 0 and s % tm == 0 and width % tn == 0
    pad_blocks = PAD_ROWS // tm
    n_u = width // tn
    grid = (bsz, s // tm + pad_blocks, n_u + 1)
    row = lambda bi, i, j: (bi, jnp.maximum(i - pad_blocks, 0), 0)
    const = lambda shape: pl.BlockSpec(shape, lambda bi, i, j: (0,) * len(shape), pipeline_mode=pl.Buffered(1))
    return pl.pallas_call(
        functools.partial(_inproj_kernel, apply_ln=apply_ln, n_u=n_u, tn=tn),
        grid=grid,
        in_specs=[
            pl.BlockSpec((1, tm, d), row),
            const((1, d)), const((1, d)),
            const((d, width)),
            const((aw, d)),
        ],
        out_specs=[
            pl.BlockSpec((1, tm, d), row),
            pl.BlockSpec((1, tm, tn), lambda bi, i, j: (bi, i, jnp.minimum(j, n_u - 1))),
            pl.BlockSpec((1, aw, tm), lambda bi, i, j: (bi, 0, i)),
        ],
        out_shape=[
            jax.ShapeDtypeStruct((bsz, s, d), F32),
            jax.ShapeDtypeStruct((bsz, s + PAD_ROWS, width), BF16),
            jax.ShapeDtypeStruct((bsz, aw, s + PAD_ROWS), BF16),
        ],
        scratch_shapes=[pltpu.VMEM((tm, d), BF16)],
        compiler_params=_cparams(("arbitrary", "arbitrary", "arbitrary")),
        name="inproj",
    )(x, g, b, w_u, w_vt)


def _conv_kernel(ah_ref, gh_ref, at_ref, gt_ref, w_ref, cb_ref, lg_ref, lb_ref, o_ref, v_scr, y_scr, *, ts, c):
    v_scr[0:HALO_ROWS, :] = ah_ref[0].astype(F32) * jax.nn.sigmoid(gh_ref[0].astype(F32))
    v_scr[HALO_ROWS:HALO_ROWS + ts, :] = at_ref[0].astype(F32) * jax.nn.sigmoid(gt_ref[0].astype(F32))
    first_tap = HALO_ROWS - (CONV_WIDTH - 1)
    win_rows = CONV_ROW_CHUNK + HALO_ROWS

    def row_chunk(rc, carry):
        r0 = pl.multiple_of(rc * CONV_ROW_CHUNK, CONV_ROW_CHUNK)
        for ci in range(c // LANES):
            lanes = slice(ci * LANES, (ci + 1) * LANES)
            win = v_scr[pl.ds(r0, win_rows), lanes]
            shifted = [win] + [pltpu.roll(win, win_rows - sh, 0) for sh in range(1, SUBLANES)]
            wc = w_ref[:, lanes]
            acc = jnp.zeros((CONV_ROW_CHUNK, LANES), F32)
            for k in range(CONV_WIDTH):
                start = (first_tap + k) // SUBLANES * SUBLANES
                acc = acc + wc[k:k + 1, :] * shifted[(first_tap + k) % SUBLANES][start:start + CONV_ROW_CHUNK, :]
            y_scr[pl.ds(r0, CONV_ROW_CHUNK), lanes] = acc + cb_ref[:, lanes]
        return carry

    lax.fori_loop(0, ts // CONV_ROW_CHUNK, row_chunk, 0)
    y = _layer_norm(y_scr[...], lg_ref[...], lb_ref[...])
    o_ref[0] = (y * jax.nn.sigmoid(y)).astype(BF16)


def _conv(u, conv_w, conv_b, ln_g, ln_b, *, s, c, ts):
    bsz = u.shape[0]
    assert PAD_ROWS % ts == 0 and s % ts == 0 and ts % HALO_ROWS == 0 and ts % CONV_ROW_CHUNK == 0
    hb = ts // HALO_ROWS
    pb = PAD_ROWS // ts
    halo = lambda col: pl.BlockSpec((1, HALO_ROWS, c), lambda bi, i: (bi, (pb + i) * hb - 1, col))
    tile = lambda col: pl.BlockSpec((1, ts, c), lambda bi, i: (bi, pb + i, col))
    vec = lambda rows: pl.BlockSpec((rows, c), lambda bi, i: (0, 0))
    return pl.pallas_call(
        functools.partial(_conv_kernel, ts=ts, c=c),
        grid=(bsz, s // ts),
        in_specs=[halo(0), halo(1), tile(0), tile(1), vec(CONV_WIDTH), vec(1), vec(1), vec(1)],
        out_specs=pl.BlockSpec((1, ts, c), lambda bi, i: (bi, i, 0)),
        out_shape=jax.ShapeDtypeStruct((bsz, s, c), BF16),
        scratch_shapes=[pltpu.VMEM((ts + HALO_ROWS, c), F32), pltpu.VMEM((ts, c), F32)],
        compiler_params=_cparams(("arbitrary", "arbitrary")),
        name="conv",
    )(u, u, u, u, conv_w, conv_b, ln_g, ln_b)


def _attn_kernel(q_ref, k0_ref, k1_ref, k2_ref, v0_ref, v1_ref, v2_ref, bias_ref, o_ref, kw_scr, vw_scr, st_scr, p_scr,
                 *, qb, aw):
    i = pl.program_id(1)
    for j, (kr, vr) in enumerate(((k0_ref, v0_ref), (k1_ref, v1_ref), (k2_ref, v2_ref))):
        kw_scr[j * qb:(j + 1) * qb, :] = kr[0]
        vw_scr[:, j * qb:(j + 1) * qb] = vr[0]
    lane = lax.broadcasted_iota(I32, (1, LANES), 1)
    first_head = lane < HEAD_DIM
    key_iota = lax.broadcasted_iota(I32, (BAND, LANES), 0)
    scale = jnp.asarray(HEAD_DIM ** -0.5, BF16)
    zero = jnp.zeros((), BF16)

    @pl.when((pl.program_id(0) == 0) & (i == 0))
    def _():
        p_scr[...] = jnp.zeros_like(p_scr)

    def run(masked):
        def head_pairs(hq, carry):
            n_chunks = qb // CHUNK
            los = [pl.multiple_of((ATTN_PAIRS_PER_TRIP * hq + par) * LANES, LANES) for par in range(ATTN_PAIRS_PER_TRIP)]
            vtps = [vw_scr[pl.ds(lo, LANES), :] for lo in los]
            for par, lo in enumerate(los):
                qp = q_ref[0, :, pl.ds(lo, LANES)] * scale
                bias = bias_ref[ATTN_PAIRS_PER_TRIP * hq + par]
                for ca in range(n_chunks):
                    qc = qp[ca * CHUNK:(ca + 1) * CHUNK]
                    qs = jnp.concatenate([jnp.where(first_head, qc, zero), jnp.where(first_head, zero, qc)], axis=0)
                    st = lax.dot_general(kw_scr[ca * CHUNK:ca * CHUNK + BAND, pl.ds(lo, LANES)], qs,
                                         (((1,), (1,)), ((), ())), preferred_element_type=F32) + bias
                    if masked:
                        st = jnp.where(key_iota >= PAD_ROWS - (i * qb + ca * CHUNK), st, NEG_BIG)
                    st_scr[par * n_chunks + ca] = st
            inv_l = []
            for u in range(ATTN_PAIRS_PER_TRIP * n_chunks):
                ca = u % n_chunks
                st = st_scr[u]
                m = jnp.max(st, axis=0, keepdims=True)
                e = jnp.exp(st - m)
                inv_l.append(1.0 / jnp.sum(e, axis=0, keepdims=True))
                p_scr[u, ca * CHUNK:ca * CHUNK + BAND, :] = e.astype(BF16)
            for u in range(ATTN_PAIRS_PER_TRIP * n_chunks):
                par, ca = divmod(u, n_chunks)
                ot = jnp.dot(vtps[par], p_scr[u], preferred_element_type=F32) * inv_l[u]
                t = ot.T
                o_ref[0, ca * CHUNK:(ca + 1) * CHUNK, pl.ds(los[par], LANES)] = jnp.where(
                    first_head, t[0:CHUNK], t[CHUNK:]).astype(BF16)
            return carry

        lax.fori_loop(0, aw // LANES // ATTN_PAIRS_PER_TRIP, head_pairs, 0)

    first_valid_block = PAD_ROWS // qb

    @pl.when(i < first_valid_block)
    def _():
        run(True)

    @pl.when(i >= first_valid_block)
    def _():
        run(False)


def _attention(u, vt, bias_t, *, s, aw, qb, qcol):
    bsz = u.shape[0]
    assert PAD_ROWS == 2 * qb and s % qb == 0 and aw % LANES == 0
    pb = PAD_ROWS // qb
    kspec = lambda j: pl.BlockSpec((1, qb, aw), lambda bi, i: (bi, i + j, qcol + 1))
    vspec = lambda j: pl.BlockSpec((1, aw, qb), lambda bi, i: (bi, 0, i + j))
    return pl.pallas_call(
        functools.partial(_attn_kernel, qb=qb, aw=aw),
        grid=(bsz, s // qb),
        in_specs=[pl.BlockSpec((1, qb, aw), lambda bi, i: (bi, pb + i, qcol))]
        + [kspec(j) for j in range(3)] + [vspec(j) for j in range(3)]
        + [pl.BlockSpec(bias_t.shape, lambda bi, i: (0, 0, 0))],
        out_specs=pl.BlockSpec((1, qb, aw), lambda bi, i: (bi, i, 0)),
        out_shape=jax.ShapeDtypeStruct((bsz, s, aw), BF16),
        scratch_shapes=[pltpu.VMEM((3 * qb, aw), BF16), pltpu.VMEM((aw, 3 * qb), BF16),
                        pltpu.VMEM((ATTN_PAIRS_PER_TRIP * qb // CHUNK, BAND, LANES), F32),
                        pltpu.VMEM((ATTN_PAIRS_PER_TRIP * qb // CHUNK, 3 * qb, LANES), BF16)],
        compiler_params=_cparams(("arbitrary", "arbitrary")),
        name="attn",
    )(u, u, u, u, vt, vt, vt, bias_t)


def _expand_rel_bias(rel_bias):
    tail = jnp.broadcast_to(rel_bias[:, -1:], (rel_bias.shape[0], PAD_ROWS + CHUNK - 1 - MAX_REL))
    rev = jnp.concatenate([rel_bias, tail], axis=1)[:, ::-1].astype(F32)
    bias = jnp.stack([rev[:, CHUNK - 1 - i:CHUNK - 1 - i + BAND] for i in range(CHUNK)], axis=1)
    heads = bias.shape[0]
    return bias.reshape(heads // 2, 2, CHUNK, BAND).transpose(0, 3, 1, 2).reshape(heads // 2, BAND, 2 * CHUNK)


def _row_tiles(h):
    d = h.shape[1]
    chunks = jnp.stack([h[:, ci * LANES:(ci + 1) * LANES].astype(BF16) for ci in range(d // LANES)], axis=0)
    return jnp.swapaxes(chunks, 0, 1)


def _outproj_kernel(conv_ref, att_ref, h_ref, wo_ref, g_ref, b_ref, wr_ref, h1_ref, h1r_ref, lg_ref, *, alpha, c):
    mixed = jnp.dot(conv_ref[...], wo_ref[0:c, :], preferred_element_type=F32)
    mixed = mixed + jnp.dot(att_ref[...], wo_ref[c:, :], preferred_element_type=F32)
    h1 = _layer_norm(alpha * h_ref[...] + mixed, g_ref[...], b_ref[...])
    h1b = h1.astype(BF16)
    lg_ref[...] = lax.dot_general(wr_ref[...], h1b, (((1,), (1,)), ((), ())), preferred_element_type=F32)
    h1_ref[...] = h1
    h1r_ref[...] = _row_tiles(h1)


def _outproj(conv_out, att_out, h, wo, g, b, wr_t, *, alpha, tm):
    n, d = h.shape
    c = conv_out.shape[1]
    rpt = d // LANES
    const = lambda shape: pl.BlockSpec(shape, lambda i: (0,) * len(shape), pipeline_mode=pl.Buffered(1))
    return pl.pallas_call(
        functools.partial(_outproj_kernel, alpha=alpha, c=c),
        grid=(n // tm,),
        in_specs=[
            pl.BlockSpec((tm, c), lambda i: (i, 0)),
            pl.BlockSpec((tm, att_out.shape[1]), lambda i: (i, 0)),
            pl.BlockSpec((tm, d), lambda i: (i, 0)),
            const(wo.shape), const((1, d)), const((1, d)), const(wr_t.shape),
        ],
        out_specs=[
            pl.BlockSpec((tm, d), lambda i: (i, 0)),
            pl.BlockSpec((tm, rpt, LANES), lambda i: (i, 0, 0)),
            pl.BlockSpec((N_EXPERTS, tm), lambda i: (0, i)),
        ],
        out_shape=[
            jax.ShapeDtypeStruct((n, d), F32),
            jax.ShapeDtypeStruct((n, rpt, LANES), BF16),
            jax.ShapeDtypeStruct((N_EXPERTS, n), F32),
        ],
        compiler_params=_cparams(("arbitrary",)),
        name="outproj",
    )(conv_out, att_out, h, wo, g, b, wr_t)


def _router_kernel(lg_ref, rb_ref, tri_ref, pos_ref, gd_ref, cnt_ref, carry_scr, *, tt):
    i = pl.program_id(0)

    @pl.when(i == 0)
    def _():
        carry_scr[...] = jnp.zeros_like(carry_scr)

    scores = jax.nn.sigmoid(lg_ref[...])
    sel = scores + rb_ref[...]
    sel3 = sel.reshape(GROUP_SIZE, N_GROUPS, tt)
    jio = lax.broadcasted_iota(I32, (GROUP_SIZE, N_GROUPS, tt), 0)
    m1 = jnp.max(sel3, axis=0)
    first = jnp.min(jnp.where(sel3 == m1[None], jio, GROUP_SIZE), axis=0)
    m2 = jnp.max(jnp.where(jio == first[None], -jnp.inf, sel3), axis=0)
    gs = m1 + m2
    gio = lax.broadcasted_iota(I32, (N_GROUPS, tt), 0)
    grank = jnp.zeros((N_GROUPS, tt), I32)
    for gp in range(N_GROUPS):
        row = gs[gp:gp + 1, :]
        tie = jnp.where(gio > gp, 1, 0)
        grank = grank + jnp.where(row > gs, 1, jnp.where(row == gs, tie, 0))
    gmask = grank < TOPK_GROUPS
    masked = jnp.where(gmask[None], sel3, -jnp.inf).reshape(N_EXPERTS, tt)
    rio = lax.broadcasted_iota(I32, (N_EXPERTS, tt), 0)
    eid = (rio % N_GROUPS) * GROUP_SIZE + rio // N_GROUPS
    erank = jnp.zeros((N_EXPERTS, tt), I32)
    for rp in range(N_EXPERTS):
        ep = (rp % N_GROUPS) * GROUP_SIZE + rp // N_GROUPS
        row = masked[rp:rp + 1, :]
        tie = jnp.where(eid > ep, 1, 0)
        erank = erank + jnp.where(row > masked, 1, jnp.where(row == masked, tie, 0))
    smask = erank < TOP_K
    picked = jnp.where(smask, scores, 0.0)
    wsum = jnp.sum(picked, axis=0, keepdims=True)
    gd_ref[...] = picked / wsum * ROUTED_SCALE
    ones = jnp.where(smask, 1.0, 0.0)
    incl = jnp.dot(ones.astype(BF16), tri_ref[...], preferred_element_type=F32)
    pos = carry_scr[...] + incl - ones
    pos_ref[...] = jnp.where(smask, pos, -1.0).astype(I32)
    carry = carry_scr[...] + jnp.sum(ones, axis=1, keepdims=True)
    carry_scr[...] = carry
    cnt_ref[...] = jnp.broadcast_to(carry, (N_EXPERTS, LANES)).astype(I32)


def _router(logits_t, rb, tri, *, tt):
    n = logits_t.shape[1]
    tile = pl.BlockSpec((N_EXPERTS, tt), lambda i: (0, i))
    return pl.pallas_call(
        functools.partial(_router_kernel, tt=tt),
        grid=(n // tt,),
        in_specs=[tile, pl.BlockSpec((N_EXPERTS, 1), lambda i: (0, 0)), pl.BlockSpec((tt, tt), lambda i: (0, 0))],
        out_specs=[tile, tile, pl.BlockSpec((N_EXPERTS, LANES), lambda i: (0, 0))],
        out_shape=[
            jax.ShapeDtypeStruct((N_EXPERTS, n), I32),
            jax.ShapeDtypeStruct((N_EXPERTS, n), F32),
            jax.ShapeDtypeStruct((N_EXPERTS, LANES), I32),
        ],
        scratch_shapes=[pltpu.VMEM((N_EXPERTS, 1), F32)],
        compiler_params=_cparams(("arbitrary",)),
        name="router",
    )(logits_t, rb, tri)


def _slots_kernel(pos_ref, gd_ref, ps_ref, ltri_ref, dest_ref, gate_ref):
    pos = pos_ref[...]
    gd = gd_ref[...]
    chosen = pos >= 0
    ones = jnp.where(chosen, 1.0, 0.0).astype(BF16)
    before = jnp.dot(ltri_ref[...], ones, preferred_element_type=F32)
    dest_full = pos + ps_ref[...]
    for k in range(TOP_K):
        hit = jnp.where(chosen, before, -1.0) == float(k)
        dest_ref[k:k + 1, :] = jnp.sum(jnp.where(hit, dest_full, 0), axis=0, keepdims=True)
        gate_ref[k:k + 1, :] = jnp.sum(jnp.where(hit, gd, 0.0), axis=0, keepdims=True)


def _slots(pos, gd, pad_start, ltri, *, tt):
    n = pos.shape[1]
    tile = pl.BlockSpec((N_EXPERTS, tt), lambda i: (0, i))
    otile = pl.BlockSpec((TOP_K, tt), lambda i: (0, i))
    return pl.pallas_call(
        _slots_kernel,
        grid=(n // tt,),
        in_specs=[tile, tile, pl.BlockSpec((N_EXPERTS, 1), lambda i: (0, 0)),
                  pl.BlockSpec((N_EXPERTS, N_EXPERTS), lambda i: (0, 0))],
        out_specs=[otile, otile],
        out_shape=[jax.ShapeDtypeStruct((TOP_K, n), I32), jax.ShapeDtypeStruct((TOP_K, n), F32)],
        compiler_params=_cparams(("arbitrary",)),
        name="slots",
    )(pos, gd, pad_start, ltri)


def _dispatch_kernel(pad_end_ref, cnt_ref, dest_ref, rows_ref, h1_ref, wsg_ref, wsu_ref, wsd_ref, xs_ref, base_ref,
                     zero_scr, zsem, sem, *, td, rb, alpha):
    i = pl.program_id(0)

    def zero_copy(r):
        return pltpu.make_async_copy(zero_scr, xs_ref.at[pl.ds(pad_end_ref[r] - rb, rb)], zsem)

    @pl.when(i == 0)
    def _():
        zero_scr[...] = jnp.zeros_like(zero_scr)

        def start(r, carry):
            @pl.when(cnt_ref[r] > 0)
            def _():
                zero_copy(r).start()
            return carry

        def wait(r, carry):
            @pl.when(cnt_ref[r] > 0)
            def _():
                zero_copy(r).wait()
            return carry

        lax.fori_loop(0, N_EXPERTS, start, 0)
        lax.fori_loop(0, N_EXPERTS, wait, 0)

    def token(t, carry):
        src = rows_ref.at[t]
        for k in range(TOP_K):
            pltpu.make_async_copy(src, xs_ref.at[dest_ref[t * TOP_K + k]], sem).start(priority=k % 2)
        return carry

    lax.fori_loop(0, td, token, 0)

    h1 = h1_ref[...]
    h1b = h1.astype(BF16)
    sg = jnp.dot(h1b, wsg_ref[...], preferred_element_type=F32)
    su = jnp.dot(h1b, wsu_ref[...], preferred_element_type=F32)
    hid = (sg * jax.nn.sigmoid(sg) * su).astype(BF16)
    base_ref[...] = alpha * h1 + jnp.dot(hid, wsd_ref[...], preferred_element_type=F32)

    for k in range(TOP_K):
        pltpu.make_async_copy(rows_ref, xs_ref.at[pl.ds(0, td)], sem).wait()


def _dispatch(pad_end, counts, dest_flat, h1r, h1, wsg, wsu, wsd, *, n_rows, td, rb, alpha):
    n, rpt, _ = h1r.shape
    d = h1.shape[1]
    const = lambda shape: pl.BlockSpec(shape, lambda i, pe, cn: (0,) * len(shape), pipeline_mode=pl.Buffered(1))
    grid_spec = pltpu.PrefetchScalarGridSpec(
        num_scalar_prefetch=2,
        grid=(n // td,),
        in_specs=[
            pl.BlockSpec((td * TOP_K,), lambda i, pe, cn: (i,), memory_space=pltpu.SMEM),
            pl.BlockSpec((td, rpt, LANES), lambda i, pe, cn: (i, 0, 0)),
            pl.BlockSpec((td, d), lambda i, pe, cn: (i, 0)),
            const(wsg.shape), const(wsu.shape), const(wsd.shape),
        ],
        out_specs=[pl.BlockSpec(memory_space=pl.ANY), pl.BlockSpec((td, d), lambda i, pe, cn: (i, 0))],
        scratch_shapes=[pltpu.VMEM((rb, rpt, LANES), BF16), pltpu.SemaphoreType.DMA(()), pltpu.SemaphoreType.DMA(())],
    )
    return pl.pallas_call(
        functools.partial(_dispatch_kernel, td=td, rb=rb, alpha=alpha),
        grid_spec=grid_spec,
        out_shape=[jax.ShapeDtypeStruct((n_rows, rpt, LANES), BF16), jax.ShapeDtypeStruct((n, d), F32)],
        compiler_params=_cparams(("arbitrary",)),
        name="dispatch",
    )(pad_end, counts, dest_flat, h1r, h1, wsg, wsu, wsd)


def _expert_kernel(bidx_ref, bexp_ref, bnext_ref, nused_ref, xs_ref, wg_hbm, wu_hbm, wd_hbm, y_ref,
                   x_scr, wg_f32, wu_f32, wd_f32, wg_scr, wu_scr, wd_scr, wsem, *, rb, d):
    b = pl.program_id(0)

    def weight_copies(e):
        return (pltpu.make_async_copy(wg_hbm.at[e], wg_f32, wsem.at[0]),
                pltpu.make_async_copy(wu_hbm.at[e], wu_f32, wsem.at[1]),
                pltpu.make_async_copy(wd_hbm.at[e], wd_f32, wsem.at[2]))

    @pl.when(b == 0)
    def _():
        for cp in weight_copies(bexp_ref[0]):
            cp.start()

    new_expert = (b == 0) | (bexp_ref[b] != bexp_ref[jnp.maximum(b - 1, 0)])

    @pl.when(new_expert)
    def _():
        for cp in weight_copies(bexp_ref[b]):
            cp.wait()
        wg_scr[...] = wg_f32[...].astype(BF16)
        wu_scr[...] = wu_f32[...].astype(BF16)
        wd_scr[...] = wd_f32[...].astype(BF16)

        @pl.when(bnext_ref[b] >= 0)
        def _():
            for cp in weight_copies(bnext_ref[b]):
                cp.start()

    @pl.when(b < nused_ref[0])
    def _():
        xt = jnp.swapaxes(xs_ref[...], 0, 1)
        for ci in range(d // LANES):
            x_scr[:, ci * LANES:(ci + 1) * LANES] = xt[ci]
        x = x_scr[...]
        g = jnp.dot(x, wg_scr[...], preferred_element_type=F32)
        u = jnp.dot(x, wu_scr[...], preferred_element_type=F32)
        hid = (g * jax.nn.sigmoid(g) * u).astype(BF16)
        y = jnp.dot(hid, wd_scr[...], preferred_element_type=F32)
        y_ref[...] = _row_tiles(y)


def _experts(bidx, bexp, bnext, nused, xs, wg, wu, wd, *, rb, d):
    ypt = d // LANES
    n_blocks = xs.shape[0] // rb
    f = wg.shape[2]
    grid_spec = pltpu.PrefetchScalarGridSpec(
        num_scalar_prefetch=4,
        grid=(n_blocks,),
        in_specs=[
            pl.BlockSpec((rb, ypt, LANES), lambda b, bi, be, bn, nu: (bi[b], 0, 0)),
            pl.BlockSpec(memory_space=pl.ANY), pl.BlockSpec(memory_space=pl.ANY), pl.BlockSpec(memory_space=pl.ANY),
        ],
        out_specs=pl.BlockSpec((rb, ypt, LANES), lambda b, bi, be, bn, nu: (bi[b], 0, 0)),
        scratch_shapes=[pltpu.VMEM((rb, d), BF16),
                        pltpu.VMEM((d, f), F32), pltpu.VMEM((d, f), F32), pltpu.VMEM((f, d), F32),
                        pltpu.VMEM((d, f), BF16), pltpu.VMEM((d, f), BF16), pltpu.VMEM((f, d), BF16),
                        pltpu.SemaphoreType.DMA((3,))],
    )
    return pl.pallas_call(
        functools.partial(_expert_kernel, rb=rb, d=d),
        grid_spec=grid_spec,
        out_shape=jax.ShapeDtypeStruct((n_blocks * rb, ypt, LANES), BF16),
        compiler_params=_cparams(("arbitrary",)),
        name="experts",
    )(bidx, bexp, bnext, nused, xs, wg, wu, wd)


def _combine_kernel(dest_ref, gate_ref, base_ref, g_ref, b_ref, y_ref, o_ref, buf, ys_scr, r_scr, sem, *, tc, ypt, nt):
    i = pl.program_id(0)
    slot = i % 2

    @pl.when(i < nt)
    def _():
        def token(t, carry):
            for k in range(TOP_K):
                pltpu.make_async_copy(y_ref.at[dest_ref[t * TOP_K + k]], buf.at[slot * TOP_K + k, t],
                                      sem.at[slot]).start(priority=k % 2)
            return carry

        lax.fori_loop(0, tc, token, 0)

    @pl.when(i > 0)
    def _():
        ps = 1 - slot
        rows_like = buf.at[slot * TOP_K]
        for k in range(TOP_K):
            pltpu.make_async_copy(rows_like, buf.at[ps * TOP_K + k], sem.at[ps]).wait()
        gate = gate_ref[...]
        gk = [jnp.broadcast_to(gate[:, k:k + 1], (tc, LANES)) for k in range(TOP_K)]
        for k in range(TOP_K):
            yk = jnp.swapaxes(buf[ps * TOP_K + k], 0, 1)
            for ci in range(ypt):
                ys_scr[k, :, ci * LANES:(ci + 1) * LANES] = yk[ci]
        for ci in range(ypt):
            lanes = slice(ci * LANES, (ci + 1) * LANES)
            acc = base_ref[:, lanes]
            for k in range(TOP_K):
                acc = acc + gk[k] * ys_scr[k, :, lanes].astype(F32)
            r_scr[:, lanes] = acc
        o_ref[...] = _layer_norm(r_scr[...], g_ref[...], b_ref[...])


def _combine(dest_flat, gate_t, base, g, b, y, *, tc):
    n, d = base.shape
    ypt = d // LANES
    nt = n // tc
    prev = lambda i: (jnp.maximum(i - 1, 0), 0)
    return pl.pallas_call(
        functools.partial(_combine_kernel, tc=tc, ypt=ypt, nt=nt),
        grid=(nt + 1,),
        in_specs=[
            pl.BlockSpec((tc * TOP_K,), lambda i: (jnp.minimum(i, nt - 1),), memory_space=pltpu.SMEM),
            pl.BlockSpec((tc, TOP_K), prev),
            pl.BlockSpec((tc, d), prev),
            pl.BlockSpec((1, d), lambda i: (0, 0)),
            pl.BlockSpec((1, d), lambda i: (0, 0)),
            pl.BlockSpec(memory_space=pl.ANY),
        ],
        out_specs=pl.BlockSpec((tc, d), prev),
        out_shape=jax.ShapeDtypeStruct((n, d), F32),
        scratch_shapes=[
            pltpu.VMEM((2 * TOP_K, tc, ypt, LANES), BF16),
            pltpu.VMEM((TOP_K, tc, d), BF16),
            pltpu.VMEM((tc, d), F32),
            pltpu.SemaphoreType.DMA((2,)),
        ],
        compiler_params=_cparams(("arbitrary",)),
        name="combine",
    )(dest_flat, gate_t, base, g, b, y)


def _row_vec(v):
    return v.reshape(1, -1).astype(F32)


def _expert_rows(a):
    return a.reshape((N_GROUPS, GROUP_SIZE) + a.shape[1:]).swapaxes(0, 1).reshape(a.shape)


def _tile(n, want):
    t = min(n, want)
    assert n % t == 0
    return t


def kernel(x, ln_in_g, ln_in_b, w_in, conv_w, conv_b, conv_ln_g, conv_ln_b, rel_bias, w_out, ln1_g, ln1_b,
           w_router, router_bias, w_gate, w_up, w_down, w_shared_gate, w_shared_up, w_shared_down, ln2_g, ln2_b):
    bsz, s, d = x.shape
    n = bsz * s
    depth = w_in.shape[0]
    c = conv_w.shape[2]
    aw = rel_bias.shape[1] * HEAD_DIM
    assert c == aw and w_in.shape[2] == 2 * c + 3 * aw and w_router.shape[2] == N_EXPERTS
    alpha = (2 * depth) ** 0.25
    rb = EXPERT_ROW_BLOCK
    n_blocks = -(-n * TOP_K // rb) + N_EXPERTS
    n_rows = n_blocks * rb
    tt = _tile(n, 1024)
    tri = jnp.asarray(np.triu(np.ones((tt, tt), np.float32)), BF16)
    ltri = jnp.asarray(np.tril(np.ones((N_EXPERTS, N_EXPERTS), np.float32), -1), BF16)

    h = x
    for li in range(depth):
        hn, u, vt = _inproj(h, _row_vec(ln_in_g), _row_vec(ln_in_b), w_in[li, :, :2 * c + 2 * aw].astype(BF16),
                            w_in[li, :, 2 * c + 2 * aw:].T.astype(BF16), apply_ln=(li == 0), tm=_tile(s, 512), tn=c)
        conv_out = _conv(u, conv_w[li].astype(F32), _row_vec(conv_b[li]), _row_vec(conv_ln_g[li]),
                         _row_vec(conv_ln_b[li]), s=s, c=c, ts=_tile(s, 512))
        att_out = _attention(u, vt, _expand_rel_bias(rel_bias[li]), s=s, aw=aw, qb=PAD_ROWS // 2, qcol=2)
        wr_t = _expert_rows(w_router[li].T).astype(BF16)
        h1, h1r, logits_t = _outproj(
            conv_out.reshape(n, c), att_out.reshape(n, aw), hn.reshape(n, d), w_out[li].astype(BF16),
            _row_vec(ln1_g[li]), _row_vec(ln1_b[li]), wr_t, alpha=alpha, tm=_tile(n, 512))
        rbias = _expert_rows(router_bias[li].reshape(N_EXPERTS, 1)).astype(F32)
        pos, gd, cnt = _router(logits_t, rbias, tri, tt=tt)
        counts = cnt[:, 0]
        padded = (counts + rb - 1) // rb * rb
        pad_end = jnp.cumsum(padded).astype(I32)
        pad_start = pad_end - padded
        dest, gate = _slots(pos, gd, pad_start.reshape(N_EXPERTS, 1), ltri, tt=tt)
        dest_flat = dest.T.reshape(n * TOP_K)
        xs, base = _dispatch(pad_end, counts, dest_flat, h1r, h1,
                             w_shared_gate[li].astype(BF16), w_shared_up[li].astype(BF16),
                             w_shared_down[li].astype(BF16), n_rows=n_rows, td=_tile(n, 256), rb=rb, alpha=alpha)
        n_used = pad_end[-1] // rb
        blk = jnp.minimum(jnp.arange(n_blocks, dtype=I32), n_used - 1)
        blk_row = jnp.sum((pad_end[None, :] <= (blk * rb)[:, None]).astype(I32), axis=1)
        expert_of_row = lambda r: (r % N_GROUPS) * GROUP_SIZE + r // N_GROUPS
        row_ids = jnp.arange(N_EXPERTS, dtype=I32)
        candidate = (row_ids[None, :] > blk_row[:, None]) & (counts[None, :] > 0)
        next_row = jnp.min(jnp.where(candidate, row_ids[None, :], N_EXPERTS), axis=1)
        blk_next = jnp.where(next_row < N_EXPERTS, expert_of_row(next_row), -1)
        y = _experts(blk, expert_of_row(blk_row), blk_next, n_used.reshape(1), xs,
                     w_gate[li], w_up[li], w_down[li], rb=rb, d=d)
        h = _combine(dest_flat, gate.T, base, _row_vec(ln2_g[li]), _row_vec(ln2_b[li]), y,
                     tc=_tile(n, 128)).reshape(bsz, s, d)
    return h
```
